```python
import jax, jax.numpy as jnp
from jax import lax
import numpy as np

D_MODEL = 2048
BATCH = 4
SEQ = 4096
DEPTH = 1

MEM_LEN = 256
POOL_WIDTH = 1024
POOL_WINDOWS = (2, 4, 8, 16)
POOL_GROUPS = len(POOL_WINDOWS)
POOL_GROUP = POOL_WIDTH // POOL_GROUPS
ATTN_HEADS = 8
HEAD_DIM = 128
ATTN_WIDTH = ATTN_HEADS * HEAD_DIM
MIX_WIDTH = POOL_WIDTH + ATTN_WIDTH
IN_COLS = POOL_WIDTH + 3 * ATTN_WIDTH
DILATED_PAIRS = ((128, 1), (512, 4), (2048, 16))
BLOCK = 128
ROPE_THETA = 500000.0
ROPE_DIM = HEAD_DIM // 4
CROSS_HEADS = 4
CROSS_HEAD_DIM = 128
CROSS_WIDTH = CROSS_HEADS * CROSS_HEAD_DIM
PEER_KEYS = 128
PEER_EXPERTS = PEER_KEYS * PEER_KEYS
PEER_HEADS = 8
PEER_QUERY_DIM = 256
PEER_HALF = PEER_QUERY_DIM // 2
PEER_TOPK = 16
PEER_TOKEN_BLOCK = 128
EPS = 1e-6

kernel_name = "hybrid_pool_dilated_peer_block"


def rmsnorm(x, g):
    xf = x.astype(jnp.float32)
    y = xf * lax.rsqrt(jnp.mean(xf * xf, axis=-1, keepdims=True) + EPS)
    return (y * g.astype(jnp.float32)).astype(x.dtype)


def rope_partial(t, positions):
    half = ROPE_DIM // 2
    inv = ROPE_THETA ** (-jnp.arange(0, ROPE_DIM, 2, dtype=jnp.float32) / ROPE_DIM)
    ang = positions.astype(jnp.float32)[..., None] * inv
    cos = jnp.cos(ang)[:, :, None, :]
    sin = jnp.sin(ang)[:, :, None, :]
    tr = t[..., :ROPE_DIM].astype(jnp.float32)
    x1, x2 = tr[..., :half], tr[..., half:]
    rot = jnp.concatenate([x1 * cos - x2 * sin, x2 * cos + x1 * sin], axis=-1)
    return jnp.concatenate([rot.astype(t.dtype), t[..., ROPE_DIM:]], axis=-1)


def causal_pool_mixer(p, w_pool, pool_scale):
    B, S, _ = p.shape
    pg = p.reshape(B, S, POOL_GROUPS, POOL_GROUP)
    c = jnp.cumsum(pg.astype(jnp.float32), axis=1)
    t = jnp.arange(S)
    pooled = []
    for g, w in enumerate(POOL_WINDOWS):
        cg = c[:, :, g]
        prev = jnp.pad(cg, ((0, 0), (w, 0), (0, 0)))[:, :S]
        cnt = jnp.minimum(t + 1, w).astype(jnp.float32)[None, :, None]
        pooled.append((cg - prev) / cnt)
    mixed = (jnp.stack(pooled, axis=2) - pg.astype(jnp.float32)).astype(p.dtype)
    y = jnp.einsum('bsgc,gce->bsge', mixed, w_pool) * pool_scale
    return y.reshape(B, S, POOL_WIDTH)


def dilated_branch(q, k, v, window, dilation):
    B, S, H, Dh = q.shape
    L = S // dilation
    w_sub = window // dilation
    nb = -(-L // BLOCK)
    Lp = nb * BLOCK

    def to_sub(t):
        t = t.reshape(B, L, dilation, H, Dh).transpose(0, 2, 3, 1, 4).reshape(B * dilation, H, L, Dh)
        t = jnp.pad(t, ((0, 0), (0, 0), (0, Lp - L), (0, 0)))
        return t.reshape(B * dilation, H, nb, BLOCK, Dh)

    def with_prev(t):
        prev = jnp.pad(t, ((0, 0), (0, 0), (1, 0), (0, 0), (0, 0)))[:, :, :nb]
        return jnp.concatenate([prev, t], axis=3)

    qs = to_sub(q)
    kk = with_prev(to_sub(k))
    vv = with_prev(to_sub(v))
    s = jnp.einsum('ghnqd,ghnkd->ghnqk', qs, kk).astype(jnp.float32)
    ql = jnp.arange(BLOCK)[:, None]
    kl = jnp.arange(2 * BLOCK)[None, :]
    dist = ql + BLOCK - kl
    key_idx = jnp.arange(nb)[:, None, None] * BLOCK - BLOCK + kl[None]
    mask = (dist >= 0) & (dist <= w_sub) & (key_idx >= 0)
    s = jnp.where(mask, s, -jnp.inf)
    m = jnp.max(s, axis=-1, keepdims=True)
    pe = jnp.exp(s - m)
    l = jnp.sum(pe, axis=-1, keepdims=True)
    o = jnp.einsum('ghnqk,ghnkd->ghnqd', pe, vv.astype(jnp.float32)) / l
    lse = (m + jnp.log(l))[..., 0]
    o = o.reshape(B, dilation, H, Lp, Dh)[:, :, :, :L].transpose(0, 3, 1, 2, 4).reshape(B, S, H, Dh)
    lse = lse.reshape(B, dilation, H, Lp)[..., :L].transpose(0, 3, 1, 2).reshape(B, S, H)
    return o, lse


def dilated_attention(q, k, v):
    outs, lses = [], []
    for window, dilation in DILATED_PAIRS:
        o, lse = dilated_branch(q, k, v, window, dilation)
        outs.append(o)
        lses.append(lse)
    wts = jax.nn.softmax(jnp.stack(lses, axis=-1), axis=-1)
    o = sum(wts[..., i, None] * outs[i] for i in range(len(outs)))
    return o.astype(q.dtype)


def memory_cross_attention(h, memn, w_cq, w_ck, w_cv, w_co):
    B, S, _ = h.shape
    M = memn.shape[1]
    q = (h @ w_cq).reshape(B, S, CROSS_HEADS, CROSS_HEAD_DIM) * (CROSS_HEAD_DIM ** -0.5)
    k = (memn @ w_ck).reshape(B, M, CROSS_HEADS, CROSS_HEAD_DIM)
    v = (memn @ w_cv).reshape(B, M, CROSS_HEADS, CROSS_HEAD_DIM)
    s = jnp.einsum('bshd,bmhd->bhsm', q, k).astype(jnp.float32)
    p = jax.nn.softmax(s, axis=-1).astype(v.dtype)
    o = jnp.einsum('bhsm,bmhd->bshd', p, v).reshape(B, S, CROSS_WIDTH)
    return o @ w_co


def peer_ffn(h, w_pq, sub_keys_1, sub_keys_2, w_u, w_v):
    B, S, D = h.shape
    T = B * S
    hf = h.reshape(T, D)
    q = (hf @ w_pq).reshape(T, PEER_HEADS, 2, PEER_HALF)
    s1 = jnp.einsum('thd,kd->thk', q[:, :, 0], sub_keys_1).astype(jnp.float32)
    s2 = jnp.einsum('thd,kd->thk', q[:, :, 1], sub_keys_2).astype(jnp.float32)
    v1, i1 = lax.top_k(s1, PEER_TOPK)
    v2, i2 = lax.top_k(s2, PEER_TOPK)
    cand = (v1[..., :, None] + v2[..., None, :]).reshape(T, PEER_HEADS, PEER_TOPK * PEER_TOPK)
    sc, ci = lax.top_k(cand, PEER_TOPK)
    e1 = jnp.take_along_axis(i1, ci // PEER_TOPK, axis=-1)
    e2 = jnp.take_along_axis(i2, ci % PEER_TOPK, axis=-1)
    experts = e1 * PEER_KEYS + e2
    gates = jax.nn.softmax(sc, axis=-1).astype(h.dtype)
    nblk = T // PEER_TOKEN_BLOCK
    idx = experts.reshape(nblk, PEER_TOKEN_BLOCK, PEER_HEADS * PEER_TOPK)
    gts = gates.reshape(nblk, PEER_TOKEN_BLOCK, PEER_HEADS * PEER_TOPK)

    def expert_block(args):
        hb, ib, gb = args
        u = jnp.take(w_u, ib, axis=0)
        a = jnp.einsum('td,tkd->tk', hb, u)
        c = gb * jax.nn.gelu(a)
        vv = jnp.take(w_v, ib, axis=0)
        return jnp.einsum('tk,tkd->td', c, vv)

    y = lax.map(expert_block, (hf.reshape(nblk, PEER_TOKEN_BLOCK, D), idx, gts))
    return y.reshape(B, S, D)


def setup_inputs(seed: int = 0) -> dict:
    key = jax.random.key(seed)
    ks = jax.random.split(key, 24)
    f32 = jnp.float32

    def nrm(k, shape, scale):
        return jax.random.normal(k, shape, f32) * scale

    def gain(k, shape):
        return 1.0 + 0.02 * jax.random.normal(k, shape, f32)

    offset = jax.random.randint(ks[2], (BATCH, 1), 0, 1024, dtype=jnp.int32)
    positions = offset + jnp.arange(SEQ, dtype=jnp.int32)[None, :]
    return {
        "x": nrm(ks[0], (BATCH, SEQ, D_MODEL), 1.0),
        "mem": nrm(ks[1], (BATCH, MEM_LEN, D_MODEL), 1.0),
        "positions": positions,
        "g_mix": gain(ks[3], (DEPTH, D_MODEL)),
        "w_in": nrm(ks[4], (DEPTH, D_MODEL, IN_COLS), D_MODEL ** -0.5),
        "w_pool": nrm(ks[5], (DEPTH, POOL_GROUPS, POOL_GROUP, POOL_GROUP), POOL_GROUP ** -0.5),
        "pool_scale": 1.0 + 0.1 * jax.random.normal(ks[6], (DEPTH, POOL_GROUPS, POOL_GROUP), f32),
        "w_out": nrm(ks[7], (DEPTH, MIX_WIDTH, D_MODEL), MIX_WIDTH ** -0.5),
        "g_cross": gain(ks[8], (DEPTH, D_MODEL)),
        "g_mem": gain(ks[9], (DEPTH, D_MODEL)),
        "w_cq": nrm(ks[10], (DEPTH, D_MODEL, CROSS_WIDTH), D_MODEL ** -0.5),
        "w_ck": nrm(ks[11], (DEPTH, D_MODEL, CROSS_WIDTH), D_MODEL ** -0.5),
        "w_cv": nrm(ks[12], (DEPTH, D_MODEL, CROSS_WIDTH), D_MODEL ** -0.5),
        "w_co": nrm(ks[13], (DEPTH, CROSS_WIDTH, D_MODEL), CROSS_WIDTH ** -0.5),
        "g_ffn": gain(ks[14], (DEPTH, D_MODEL)),
        "w_pq": nrm(ks[15], (DEPTH, D_MODEL, PEER_HEADS * PEER_QUERY_DIM), D_MODEL ** -0.5),
        "sub_keys_1": nrm(ks[16], (DEPTH, PEER_KEYS, PEER_HALF), PEER_HALF ** -0.5),
        "sub_keys_2": nrm(ks[17], (DEPTH, PEER_KEYS, PEER_HALF), PEER_HALF ** -0.5),
        "w_u": nrm(ks[18], (DEPTH, PEER_EXPERTS, D_MODEL), D_MODEL ** -0.5),
        "w_v": nrm(ks[19], (DEPTH, PEER_EXPERTS, D_MODEL), (PEER_HEADS * PEER_TOPK) ** -0.5),
        "g_final": gain(ks[20], (D_MODEL,)),
    }


def reference(x, mem, positions, g_mix, w_in, w_pool, pool_scale, w_out, g_cross, g_mem,
              w_cq, w_ck, w_cv, w_co, g_ffn, w_pq, sub_keys_1, sub_keys_2, w_u, w_v, g_final):
    B, S, _ = x.shape
    for i in range(DEPTH):
        h = rmsnorm(x, g_mix[i])
        z = h @ w_in[i]
        p = z[..., :POOL_WIDTH]
        q = z[..., POOL_WIDTH:POOL_WIDTH + ATTN_WIDTH].reshape(B, S, ATTN_HEADS, HEAD_DIM)
        k = z[..., POOL_WIDTH + ATTN_WIDTH:POOL_WIDTH + 2 * ATTN_WIDTH].reshape(B, S, ATTN_HEADS, HEAD_DIM)
        v = z[..., POOL_WIDTH + 2 * ATTN_WIDTH:].reshape(B, S, ATTN_HEADS, HEAD_DIM)
        pool_out = causal_pool_mixer(p, w_pool[i], pool_scale[i])
        q = rope_partial(q, positions) * (HEAD_DIM ** -0.5)
        k = rope_partial(k, positions)
        attn_out = dilated_attention(q, k, v).reshape(B, S, ATTN_WIDTH)
        x = x + jnp.concatenate([pool_out, attn_out], axis=-1) @ w_out[i]
        memn = rmsnorm(mem, g_mem[i])
        x = x + memory_cross_attention(rmsnorm(x, g_cross[i]), memn, w_cq[i], w_ck[i], w_cv[i], w_co[i])
        x = x + peer_ffn(rmsnorm(x, g_ffn[i]), w_pq[i], sub_keys_1[i], sub_keys_2[i], w_u[i], w_v[i])
    return rmsnorm(x, g_final)
```

```python
import functools
import math

import jax
import jax.numpy as jnp
from jax import lax
from jax.experimental import pallas as pl
from jax.experimental.pallas import tpu as pltpu

D_MODEL = 2048
POOL_WIDTH = 1024
POOL_WINDOWS = (2, 4, 8, 16)
POOL_GROUP = POOL_WIDTH // len(POOL_WINDOWS)
POOL_HALO = 16
ATTN_HEADS = 8
HEAD_DIM = 128
ATTN_WIDTH = ATTN_HEADS * HEAD_DIM
DILATIONS = (1, 4, 16)
ATTN_BLOCK = 128
ROPE_THETA = 500000.0
ROPE_DIM = HEAD_DIM // 4
ROPE_HALF = ROPE_DIM // 2
CROSS_HEADS = 4
CROSS_HEAD_DIM = 128
CROSS_WIDTH = CROSS_HEADS * CROSS_HEAD_DIM
PEER_KEYS = 128
PEER_HEADS = 8
PEER_HALF = 128
PEER_TOPK = 16
EPS = 1e-6

LANES = 128
VMEM_LIMIT = 56 * 1024 * 1024

BF16 = jnp.bfloat16
F32 = jnp.float32
NEG_INF = float("-inf")
POS_INF = float("inf")


def _cparams(*sem):
    return pltpu.CompilerParams(dimension_semantics=sem, vmem_limit_bytes=VMEM_LIMIT)


def _rms(xf, g):
    return xf * lax.rsqrt(jnp.mean(xf * xf, axis=-1, keepdims=True) + EPS) * g


def _dot(a, b):
    return jnp.dot(a, b, preferred_element_type=F32)


def _dot_nt(a, b):
    return lax.dot_general(a, b, (((1,), (1,)), ((), ())), preferred_element_type=F32)


IN_TM = 512
IN_SECTIONS = 4


def _in_proj_kernel(x_ref, g_ref, pos_ref, inv_ref, w_ref, p_ref, q_ref, k_ref, v_ref, h_scr):
    s = pl.program_id(1)

    @pl.when(s == 0)
    def _():
        h_scr[...] = _rms(x_ref[...], g_ref[...]).astype(BF16)

    z = _dot(h_scr[...], w_ref[...])

    def rope(z, scale):
        ang = pos_ref[...].astype(F32) * inv_ref[...]
        cos = jnp.cos(ang)
        sin = jnp.sin(ang)
        lane = lax.broadcasted_iota(jnp.int32, ang.shape, 1)
        sin_hi = jnp.where(lane >= ROPE_HALF, sin, 0.0)
        sin_lo = jnp.where(lane < ROPE_HALF, -sin, 0.0)
        outs = []
        for h in range(ATTN_HEADS):
            t = z[:, h * HEAD_DIM:(h + 1) * HEAD_DIM]
            r = (t * cos + pltpu.roll(t, ROPE_HALF, 1) * sin_hi
                 + pltpu.roll(t, HEAD_DIM - ROPE_HALF, 1) * sin_lo)
            outs.append((r * scale).astype(BF16) if scale != 1.0 else r.astype(BF16))
        return jnp.concatenate(outs, axis=1)

    @pl.when(s == 0)
    def _():
        p_ref[...] = z

    @pl.when(s == 1)
    def _():
        q_ref[...] = rope(z, HEAD_DIM ** -0.5)

    @pl.when(s == 2)
    def _():
        k_ref[...] = rope(z, 1.0)

    @pl.when(s == 3)
    def _():
        v_ref[...] = z.astype(BF16)


def _in_proj(x2d, g, pos, inv_lane, w_in):
    T = x2d.shape[0]
    sec = w_in.shape[1] // IN_SECTIONS
    row = lambda i, s: (i, 0)
    out_spec = pl.BlockSpec((IN_TM, sec), row)
    return pl.pallas_call(
        _in_proj_kernel,
        grid=(T // IN_TM, IN_SECTIONS),
        in_specs=[
            pl.BlockSpec((IN_TM, D_MODEL), row),
            pl.BlockSpec((1, D_MODEL), lambda i, s: (0, 0)),
            pl.BlockSpec((IN_TM, 1), row),
            pl.BlockSpec((1, LANES), lambda i, s: (0, 0)),
            pl.BlockSpec((D_MODEL, sec), lambda i, s: (0, s)),
        ],
        out_specs=[out_spec, out_spec, out_spec, out_spec],
        out_shape=[
            jax.ShapeDtypeStruct((T, sec), F32),
            jax.ShapeDtypeStruct((T, sec), BF16),
            jax.ShapeDtypeStruct((T, sec), BF16),
            jax.ShapeDtypeStruct((T, sec), BF16),
        ],
        scratch_shapes=[pltpu.VMEM((IN_TM, D_MODEL), BF16)],
        compiler_params=_cparams("arbitrary", "arbitrary"),
        name="in_proj",
    )(x2d, g, pos, inv_lane, w_in)


POOL_TS = 512


def _pool_kernel(cur_ref, halo_ref, w_ref, scale_ref, o_ref):
    si = pl.program_id(1)
    cur = cur_ref[0]
    halo = jnp.where(si > 0, halo_ref[0], 0.0)
    ts = cur.shape[0]
    t = si * ts + lax.broadcasted_iota(jnp.int32, (ts, 1), 0)
    outs = []
    for g, w in enumerate(POOL_WINDOWS):
        lo, hi = g * POOL_GROUP, (g + 1) * POOL_GROUP
        cg = cur[:, lo:hi]
        acc = jnp.concatenate([halo[:, lo:hi], cg], axis=0)
        step = 1
        while step < w:
            acc = acc + pltpu.roll(acc, step, 0)
            step *= 2
        cnt = jnp.minimum(t + 1, w).astype(F32)
        mixed = acc[POOL_HALO:, :] / cnt - cg
        y = _dot(mixed.astype(BF16), w_ref[g]) * scale_ref[g:g + 1, :]
        outs.append(y.astype(BF16))
    o_ref[0] = jnp.concatenate(outs, axis=1)


def _pool(p3d, w_pool, pool_scale):
    B, S, _ = p3d.shape
    halo_blocks = POOL_TS // POOL_HALO
    return pl.pallas_call(
        _pool_kernel,
        grid=(B, S // POOL_TS),
        in_specs=[
            pl.BlockSpec((1, POOL_TS, POOL_WIDTH), lambda b, i: (b, i, 0)),
            pl.BlockSpec((1, POOL_HALO, POOL_WIDTH),
                         lambda b, i: (b, jnp.maximum(i * halo_blocks - 1, 0), 0)),
            pl.BlockSpec(w_pool.shape, lambda b, i: (0, 0, 0)),
            pl.BlockSpec(pool_scale.shape, lambda b, i: (0, 0)),
        ],
        out_specs=pl.BlockSpec((1, POOL_TS, POOL_WIDTH), lambda b, i: (b, i, 0)),
        out_shape=jax.ShapeDtypeStruct((B, S, POOL_WIDTH), BF16),
        compiler_params=_cparams("arbitrary", "arbitrary"),
        name="pool",
    )(p3d, p3d, w_pool, pool_scale)


def _attn_kernel(*refs, first, last):
    if first:
        q_ref, kc_ref, kp_ref, vc_ref, vp_ref, o_out, lse_out = refs
    elif last:
        q_ref, kc_ref, kp_ref, vc_ref, vp_ref, o_in, lse_in, o_out = refs
    else:
        q_ref, kc_ref, kp_ref, vc_ref, vp_ref, o_in, lse_in, o_out, lse_out = refs
    n = pl.program_id(2)
    rowi = lax.broadcasted_iota(jnp.int32, (ATTN_BLOCK, ATTN_BLOCK), 0)
    coli = lax.broadcasted_iota(jnp.int32, (ATTN_BLOCK, ATTN_BLOCK), 1)
    mask_c = coli <= rowi
    mask_p = jnp.logical_and(coli >= rowi, n > 0)
    for h in range(ATTN_HEADS):
        sl = slice(h * HEAD_DIM, (h + 1) * HEAD_DIM)
        q = q_ref[0, :, sl]
        s_c = jnp.where(mask_c, _dot_nt(q, kc_ref[0, :, sl]), NEG_INF)
        s_p = jnp.where(mask_p, _dot_nt(q, kp_ref[0, :, sl]), NEG_INF)
        m = jnp.maximum(jnp.max(s_c, axis=1, keepdims=True), jnp.max(s_p, axis=1, keepdims=True))
        e_c = jnp.exp(s_c - m)
        e_p = jnp.exp(s_p - m)
        l = jnp.sum(e_c, axis=1, keepdims=True) + jnp.sum(e_p, axis=1, keepdims=True)
        o = (_dot(e_c.astype(BF16), vc_ref[0, :, sl]) + _dot(e_p.astype(BF16), vp_ref[0, :, sl])) / l
        lse = jnp.broadcast_to(m + jnp.log(l), o.shape)
        if not first:
            lse0 = lse_in[0, :, sl]
            top = jnp.maximum(lse0, lse)
            w0 = jnp.exp(lse0 - top)
            w1 = jnp.exp(lse - top)
            den = w0 + w1
            o = (o_in[0, :, sl] * w0 + o * w1) / den
            lse = top + jnp.log(den)
        o_out[0, :, sl] = o.astype(o_out.dtype)
        if not last:
            lse_out[0, :, sl] = lse


def _attn_branch(q, k, v, state, dilation, first, last):
    B, S, W = q.shape
    L = S // dilation
    nb = L // ATTN_BLOCK
    view = lambda a: a.reshape(B, L, dilation * W)
    cur = pl.BlockSpec((1, ATTN_BLOCK, W), lambda b, r, n: (b, n, r))
    prev = pl.BlockSpec((1, ATTN_BLOCK, W), lambda b, r, n: (b, jnp.maximum(n - 1, 0), r))
    ins = [view(q), view(k), view(k), view(v), view(v)]
    in_specs = [cur, cur, prev, cur, prev]
    if not first:
        ins += [view(state[0]), view(state[1])]
        in_specs += [cur, cur]
    if last:
        out_shape = [jax.ShapeDtypeStruct((B, L, dilation * W), BF16)]
        out_specs = [cur]
    else:
        out_shape = [jax.ShapeDtypeStruct((B, L, dilation * W), F32)] * 2
        out_specs = [cur, cur]
    outs = pl.pallas_call(
        functools.partial(_attn_kernel, first=first, last=last),
        grid=(B, dilation, nb),
        in_specs=in_specs,
        out_specs=out_specs,
        out_shape=out_shape,
        compiler_params=_cparams("arbitrary", "arbitrary", "arbitrary"),
        name=f"attn_d{dilation}",
    )(*ins)
    return [o.reshape(B, S, W) for o in outs]


def _dilated_attention(q, k, v):
    state = None
    for idx, d in enumerate(DILATIONS):
        state = _attn_branch(q, k, v, state, d, first=idx == 0, last=idx == len(DILATIONS) - 1)
    return state[0]


def _mem_kv_kernel(mem_ref, g_ref, wk_ref, wv_ref, k_ref, v_ref):
    h = _rms(mem_ref[...], g_ref[...]).astype(BF16)
    k_ref[...] = _dot(h, wk_ref[...]).astype(BF16)
    v_ref[...] = _dot(h, wv_ref[...]).astype(BF16)


def _mem_kv(mem2d, g_mem, w_ck, w_cv):
    M = mem2d.shape[0]
    tm = 256
    full = lambda a: pl.BlockSpec(a.shape, lambda i: (0, 0))
    out_spec = pl.BlockSpec((tm, CROSS_WIDTH), lambda i: (i, 0))
    return pl.pallas_call(
        _mem_kv_kernel,
        grid=(M // tm,),
        in_specs=[pl.BlockSpec((tm, D_MODEL), lambda i: (i, 0)), full(g_mem), full(w_ck), full(w_cv)],
        out_specs=[out_spec, out_spec],
        out_shape=[jax.ShapeDtypeStruct((M, CROSS_WIDTH), BF16)] * 2,
        compiler_params=_cparams("arbitrary"),
        name="mem_kv",
    )(mem2d, g_mem, w_ck, w_cv)


MID_TM = 256


def _mid_kernel(x_ref, pool_ref, attn_ref, wo_ref, gc_ref, wq_ref, mk_ref, mv_ref, wco_ref, gf_ref,
                x2_ref, hft_ref):
    x1 = (x_ref[...] + _dot(pool_ref[...], wo_ref[0:POOL_WIDTH, :])
          + _dot(attn_ref[...], wo_ref[POOL_WIDTH:, :]))
    hc = _rms(x1, gc_ref[...]).astype(BF16)
    q = (_dot(hc, wq_ref[...]) * (CROSS_HEAD_DIM ** -0.5)).astype(BF16)
    outs = []
    for h in range(CROSS_HEADS):
        sl = slice(h * CROSS_HEAD_DIM, (h + 1) * CROSS_HEAD_DIM)
        s = _dot_nt(q[:, sl], mk_ref[0, :, sl])
        e = jnp.exp(s - jnp.max(s, axis=1, keepdims=True))
        p = e / jnp.sum(e, axis=1, keepdims=True)
        outs.append(_dot(p.astype(BF16), mv_ref[0, :, sl]).astype(BF16))
    o = jnp.concatenate(outs, axis=1)
    x2 = x1 + _dot(o, wco_ref[...])
    x2_ref[...] = x2
    hft_ref[...] = _rms(x2, gf_ref[...]).T.astype(BF16)


def _mid(x2d, pool2d, attn2d, w_out, g_cross, w_cq, mem_k, mem_v, w_co, g_ffn, seq_len):
    T = x2d.shape[0]
    blocks_per_batch = seq_len // MID_TM
    row = lambda i: (i, 0)
    full = lambda a: pl.BlockSpec(a.shape, lambda i: (0,) * a.ndim)
    mem_spec = pl.BlockSpec((1,) + mem_k.shape[1:], lambda i: (i // blocks_per_batch, 0, 0))
    return pl.pallas_call(
        _mid_kernel,
        grid=(T // MID_TM,),
        in_specs=[
            pl.BlockSpec((MID_TM, D_MODEL), row),
            pl.BlockSpec((MID_TM, POOL_WIDTH), row),
            pl.BlockSpec((MID_TM, ATTN_WIDTH), row),
            full(w_out), full(g_cross), full(w_cq), mem_spec, mem_spec, full(w_co), full(g_ffn),
        ],
        out_specs=[pl.BlockSpec((MID_TM, D_MODEL), row), pl.BlockSpec((D_MODEL, MID_TM), lambda i: (0, i))],
        out_shape=[jax.ShapeDtypeStruct((T, D_MODEL), F32), jax.ShapeDtypeStruct((D_MODEL, T), BF16)],
        compiler_params=_cparams("arbitrary"),
        name="mid",
    )(x2d, pool2d, attn2d, w_out, g_cross, w_cq, mem_k, mem_v, w_co, g_ffn)


ROUTER_TB = 256
_CAND_ROWS = tuple(PEER_TOPK // (i + 1) for i in range(8))


def _top16(s):
    row = lax.broadcasted_iota(jnp.int32, s.shape, 0)
    vals, idxs = [], []
    for _ in range(PEER_TOPK):
        m = jnp.max(s, axis=0, keepdims=True)
        idx = jnp.min(jnp.where(s == m, row, PEER_KEYS), axis=0, keepdims=True)
        vals.append(m)
        idxs.append(idx)
        s = jnp.where(row == idx, NEG_INF, s)
    return jnp.concatenate(vals, axis=0), jnp.concatenate(idxs, axis=0)


def _router_head(s1, s2):
    n = s1.shape[1]
    v1, i1 = _top16(s1)
    v2, _ = _top16(s2)
    row16 = lax.broadcasted_iota(jnp.int32, (PEER_TOPK, n), 0)
    row8 = row16[:8]
    slabs, flat = [], []
    for g, rows in enumerate(_CAND_ROWS):
        if g == 0:
            slabs.append(v1[0:1] + v2)
            flat.append(row16)
        else:
            slabs.append(jnp.where(row8 < rows, v1[g:g + 1] + v2[:8], NEG_INF))
            flat.append(g * PEER_TOPK + row8)
    slabs.append(v1[8:] + v2[0:1])
    flat.append((row8 + 8) * PEER_TOPK)
    big = PEER_TOPK * PEER_TOPK
    top = v1[0:1] + v2[0:1]
    zsum = jnp.zeros((1, n), F32)
    for _ in range(PEER_TOPK):
        m = functools.reduce(jnp.maximum, [jnp.max(c, axis=0, keepdims=True) for c in slabs])
        f = functools.reduce(jnp.minimum, [jnp.min(jnp.where(c == m, fl, big), axis=0, keepdims=True)
                                           for c, fl in zip(slabs, flat)])
        zsum = zsum + jnp.exp(m - top)
        slabs = [jnp.where(fl == f, NEG_INF, c) for c, fl in zip(slabs, flat)]
    v2_8 = v2[:8]
    thr_rank = []
    for g in range(8):
        src = v2 if g == 0 else v2_8
        lim = row16 if g == 0 else row8
        chosen = jnp.logical_and(slabs[g] == NEG_INF, lim < _CAND_ROWS[g])
        thr_rank.append(jnp.min(jnp.where(chosen, src, POS_INF), axis=0, keepdims=True))
    tail = jnp.where(slabs[8] == NEG_INF, v2[0:1], POS_INF)
    thr_rank = jnp.concatenate(thr_rank + [tail], axis=0)
    key = lax.broadcasted_iota(jnp.int32, (PEER_KEYS, n), 0)
    theta = jnp.full((PEER_KEYS, n), POS_INF, F32)
    for r in range(PEER_TOPK):
        theta = jnp.where(key == i1[r:r + 1], thr_rank[r:r + 1], theta)
    coef = jnp.exp(s1 - v1[0:1]) / zsum
    e2 = jnp.exp(s2 - v2[0:1])
    return theta, coef, e2


def _router_kernel(hft_ref, wpqt_ref, k1_ref, k2_ref, theta_ref, coef_ref, s2_ref, e2_ref, q_scr):
    q_scr[...] = _dot(wpqt_ref[...], hft_ref[...]).astype(BF16)
    tb = hft_ref.shape[1]
    for h in range(PEER_HEADS):
        base = h * 2 * PEER_HALF
        for c in range(tb // LANES):
            ls = slice(c * LANES, (c + 1) * LANES)
            s1 = _dot(k1_ref[...], q_scr[base:base + PEER_HALF, ls])
            s2 = _dot(k2_ref[...], q_scr[base + PEER_HALF:base + 2 * PEER_HALF, ls])
            theta, coef, e2 = _router_head(s1, s2)
            theta_ref[h, :, ls] = theta
            coef_ref[h, :, ls] = coef
            s2_ref[h, :, ls] = s2
            e2_ref[h, :, ls] = e2


def _router(hft, w_pqt, k1, k2):
    T = hft.shape[1]
    full = lambda a: pl.BlockSpec(a.shape, lambda i: (0, 0))
    out_spec = pl.BlockSpec((PEER_HEADS, PEER_KEYS, ROUTER_TB), lambda i: (0, 0, i))
    return pl.pallas_call(
        _router_kernel,
        grid=(T // ROUTER_TB,),
        in_specs=[pl.BlockSpec((D_MODEL, ROUTER_TB), lambda i: (0, i)), full(w_pqt), full(k1), full(k2)],
        out_specs=[out_spec] * 4,
        out_shape=[jax.ShapeDtypeStruct((PEER_HEADS, PEER_KEYS, T), F32)] * 4,
        scratch_shapes=[pltpu.VMEM((D_MODEL, ROUTER_TB), BF16)],
        compiler_params=_cparams("arbitrary"),
        name="router",
    )(hft, w_pqt, k1, k2)


PEER_TB = 512
PEER_E1 = 8
PEER_TE = PEER_E1 * PEER_KEYS


def _gelu_tanh(x):
    return 0.5 * x * (1.0 + jnp.tanh(math.sqrt(2.0 / math.pi) * (x + 0.044715 * (x * x * x))))


def _peer_kernel(hft_ref, wu_ref, wvt_ref, theta_ref, coef_ref, s2_ref, e2_ref, yt_ref, at_scr, p_scr):
    j = pl.program_id(1)

    @pl.when(j == 0)
    def _():
        yt_ref[...] = jnp.zeros_like(yt_ref)

    at_scr[...] = _dot(wu_ref[...], hft_ref[...])
    tb = hft_ref.shape[1]
    for e in range(PEER_E1):
        rows = slice(e * PEER_KEYS, (e + 1) * PEER_KEYS)
        for c in range(tb // LANES):
            ls = slice(c * LANES, (c + 1) * LANES)
            gate = jnp.zeros((PEER_KEYS, LANES), F32)
            for h in range(PEER_HEADS):
                hit = s2_ref[h, :, ls] >= theta_ref[h, e:e + 1, ls]
                gate = gate + jnp.where(hit, e2_ref[h, :, ls], 0.0) * coef_ref[h, e:e + 1, ls]
            p_scr[rows, ls] = (gate * _gelu_tanh(at_scr[rows, ls])).astype(BF16)
    yt_ref[...] += _dot(wvt_ref[...], p_scr[...])


def _peer(hft, w_u, w_vt, theta, coef, s2, e2):
    T = hft.shape[1]
    E = w_u.shape[0]
    tok = pl.BlockSpec((PEER_HEADS, PEER_KEYS, PEER_TB), lambda i, j: (0, 0, i))
    per_e1 = pl.BlockSpec((PEER_HEADS, PEER_E1, PEER_TB), lambda i, j: (0, j, i))
    return pl.pallas_call(
        _peer_kernel,
        grid=(T // PEER_TB, E // PEER_TE),
        in_specs=[
            pl.BlockSpec((D_MODEL, PEER_TB), lambda i, j: (0, i)),
            pl.BlockSpec((PEER_TE, D_MODEL), lambda i, j: (j, 0)),
            pl.BlockSpec((D_MODEL, PEER_TE), lambda i, j: (0, j)),
            per_e1, per_e1, tok, tok,
        ],
        out_specs=pl.BlockSpec((D_MODEL, PEER_TB), lambda i, j: (0, i)),
        out_shape=jax.ShapeDtypeStruct((D_MODEL, T), F32),
        scratch_shapes=[pltpu.VMEM((PEER_TE, PEER_TB), F32), pltpu.VMEM((PEER_TE, PEER_TB), BF16)],
        compiler_params=_cparams("arbitrary", "arbitrary"),
        name="peer",
    )(hft, w_u, w_vt, theta, coef, s2, e2)


FINAL_TM = 512


def _final_kernel(x2_ref, yt_ref, g_ref, o_ref, *, norm):
    x3 = x2_ref[...] + yt_ref[...].T
    o_ref[...] = _rms(x3, g_ref[...]) if norm else x3


def _final(x2, yt, g_final, norm=True):
    T = x2.shape[0]
    return pl.pallas_call(
        functools.partial(_final_kernel, norm=norm),
        grid=(T // FINAL_TM,),
        in_specs=[
            pl.BlockSpec((FINAL_TM, D_MODEL), lambda i: (i, 0)),
            pl.BlockSpec((D_MODEL, FINAL_TM), lambda i: (0, i)),
            pl.BlockSpec((1, D_MODEL), lambda i: (0, 0)),
        ],
        out_specs=pl.BlockSpec((FINAL_TM, D_MODEL), lambda i: (i, 0)),
        out_shape=jax.ShapeDtypeStruct((T, D_MODEL), F32),
        compiler_params=_cparams("arbitrary"),
        name="final",
    )(x2, yt, g_final)


def _layer(x2d, mem2d, pos, inv_lane, B, S, g_mix, w_in, w_pool, pool_scale, w_out, g_cross, g_mem,
           w_cq, w_ck, w_cv, w_co, g_ffn, w_pq, sub_keys_1, sub_keys_2, w_u, w_v):
    row = lambda g: g.reshape(1, -1)
    p, q, k, v = _in_proj(x2d, row(g_mix), pos, inv_lane, w_in.astype(BF16))
    pool = _pool(p.reshape(B, S, POOL_WIDTH), w_pool.astype(BF16), pool_scale)
    to3d = lambda a: a.reshape(B, S, ATTN_WIDTH)
    attn = _dilated_attention(to3d(q), to3d(k), to3d(v))
    mem_k, mem_v = _mem_kv(mem2d, row(g_mem), w_ck.astype(BF16), w_cv.astype(BF16))
    mem_len = mem2d.shape[0] // B
    x2, hft = _mid(x2d, pool.reshape(B * S, POOL_WIDTH), attn.reshape(B * S, ATTN_WIDTH),
                   w_out.astype(BF16), row(g_cross), w_cq.astype(BF16),
                   mem_k.reshape(B, mem_len, CROSS_WIDTH), mem_v.reshape(B, mem_len, CROSS_WIDTH),
                   w_co.astype(BF16), row(g_ffn), S)
    theta, coef, s2, e2 = _router(hft, w_pq.T.astype(BF16), sub_keys_1.astype(BF16),
                                  sub_keys_2.astype(BF16))
    yt = _peer(hft, w_u.astype(BF16), w_v.T.astype(BF16), theta, coef, s2, e2)
    return x2, yt


def kernel(x, mem, positions, g_mix, w_in, w_pool, pool_scale, w_out, g_cross, g_mem, w_cq, w_ck, w_cv,
           w_co, g_ffn, w_pq, sub_keys_1, sub_keys_2, w_u, w_v, g_final):
    B, S, D = x.shape
    depth = w_in.shape[0]
    lane = jnp.arange(LANES)
    inv = ROPE_THETA ** (-(2.0 * (lane % ROPE_HALF)).astype(F32) / ROPE_DIM)
    inv_lane = jnp.where(lane < ROPE_DIM, inv, 0.0).astype(F32).reshape(1, LANES)
    pos = positions.reshape(B * S, 1)
    x2d = x.reshape(B * S, D)
    mem2d = mem.reshape(-1, D)
    for i in range(depth):
        x2, yt = _layer(x2d, mem2d, pos, inv_lane, B, S, g_mix[i], w_in[i], w_pool[i], pool_scale[i],
                        w_out[i], g_cross[i], g_mem[i], w_cq[i], w_ck[i], w_cv[i], w_co[i], g_ffn[i],
                        w_pq[i], sub_keys_1[i], sub_keys_2[i], w_u[i], w_v[i])
        if i + 1 < depth:
            x2d = _final(x2, yt, g_final.reshape(1, D), norm=False)
    return _final(x2, yt, g_final.reshape(1, D)).reshape(B, S, D)
```

```python
import functools
import math

import jax
import jax.numpy as jnp
from jax import lax
from jax.experimental import pallas as pl
from jax.experimental.pallas import tpu as pltpu

D_MODEL = 2048
POOL_WIDTH = 1024
POOL_WINDOWS = (2, 4, 8, 16)
POOL_GROUP = POOL_WIDTH // len(POOL_WINDOWS)
POOL_HALO = 16
ATTN_HEADS = 8
HEAD_DIM = 128
ATTN_WIDTH = ATTN_HEADS * HEAD_DIM
DILATIONS = (1, 4, 16)
ATTN_BLOCK = 128
ROPE_THETA = 500000.0
ROPE_DIM = HEAD_DIM // 4
ROPE_HALF = ROPE_DIM // 2
CROSS_HEADS = 4
CROSS_HEAD_DIM = 128
CROSS_WIDTH = CROSS_HEADS * CROSS_HEAD_DIM
PEER_KEYS = 128
PEER_HEADS = 8
PEER_HALF = 128
PEER_TOPK = 16
EPS = 1e-6

LANES = 128
VMEM_LIMIT = 56 * 1024 * 1024

BF16 = jnp.bfloat16
F32 = jnp.float32
NEG_INF = float("-inf")
POS_INF = float("inf")


def _cparams(*sem):
    return pltpu.CompilerParams(dimension_semantics=sem, vmem_limit_bytes=VMEM_LIMIT)


def _rms(xf, g):
    return xf * lax.rsqrt(jnp.mean(xf * xf, axis=-1, keepdims=True) + EPS) * g


def _dot(a, b):
    return jnp.dot(a, b, preferred_element_type=F32)


def _dot_nt(a, b):
    return lax.dot_general(a, b, (((1,), (1,)), ((), ())), preferred_element_type=F32)


IN_TM = 512
IN_SECTIONS = 4


def _in_proj_kernel(x_ref, g_ref, pos_ref, inv_ref, w_ref, p_ref, q_ref, k_ref, v_ref):
    h = _rms(x_ref[...], g_ref[...]).astype(BF16)
    sec = w_ref.shape[1] // IN_SECTIONS
    ang = pos_ref[...].astype(F32) * inv_ref[...]
    cos = jnp.cos(ang)
    sin = jnp.sin(ang)
    lane = lax.broadcasted_iota(jnp.int32, ang.shape, 1)
    sin_hi = jnp.where(lane >= ROPE_HALF, sin, 0.0)
    sin_lo = jnp.where(lane < ROPE_HALF, -sin, 0.0)

    def rope(t):
        return (t * cos + pltpu.roll(t, ROPE_HALF, 1) * sin_hi
                + pltpu.roll(t, HEAD_DIM - ROPE_HALF, 1) * sin_lo)

    p_ref[...] = _dot(h, w_ref[:, 0:sec])
    v_ref[...] = _dot(h, w_ref[:, 3 * sec:4 * sec])
    zq = _dot(h, w_ref[:, sec:2 * sec])
    zk = _dot(h, w_ref[:, 2 * sec:3 * sec])
    for h_idx in range(ATTN_HEADS):
        cols = slice(h_idx * HEAD_DIM, (h_idx + 1) * HEAD_DIM)
        q_ref[:, cols] = rope(zq[:, cols]) * (HEAD_DIM ** -0.5)
        k_ref[:, cols] = rope(zk[:, cols])


def _in_proj(x2d, g, pos, inv_lane, w_in):
    T = x2d.shape[0]
    sec = w_in.shape[1] // IN_SECTIONS
    row = lambda i: (i, 0)
    const = lambda i: (0, 0)
    out_spec = pl.BlockSpec((IN_TM, sec), row)
    return pl.pallas_call(
        _in_proj_kernel,
        grid=(T // IN_TM,),
        in_specs=[
            pl.BlockSpec((IN_TM, D_MODEL), row),
            pl.BlockSpec((1, D_MODEL), const),
            pl.BlockSpec((IN_TM, 1), row),
            pl.BlockSpec((1, LANES), const),
            pl.BlockSpec(w_in.shape, const, pipeline_mode=pl.Buffered(1)),
        ],
        out_specs=[out_spec] * IN_SECTIONS,
        out_shape=[jax.ShapeDtypeStruct((T, sec), F32)] * IN_SECTIONS,
        compiler_params=_cparams("arbitrary"),
        name="in_proj",
    )(x2d, g, pos, inv_lane, w_in)


POOL_TS = 512


def _pool_kernel(cur_ref, halo_ref, w_ref, scale_ref, o_ref):
    si = pl.program_id(1)
    cur = cur_ref[0]
    halo = jnp.where(si > 0, halo_ref[0], 0.0)
    ts = cur.shape[0]
    t = si * ts + lax.broadcasted_iota(jnp.int32, (ts, 1), 0)
    outs = []
    for g, w in enumerate(POOL_WINDOWS):
        lo, hi = g * POOL_GROUP, (g + 1) * POOL_GROUP
        cg = cur[:, lo:hi]
        acc = jnp.concatenate([halo[:, lo:hi], cg], axis=0)
        step = 1
        while step < w:
            acc = acc + pltpu.roll(acc, step, 0)
            step *= 2
        cnt = jnp.minimum(t + 1, w).astype(F32)
        mixed = acc[POOL_HALO:, :] / cnt - cg
        y = _dot(mixed.astype(BF16), w_ref[g]) * scale_ref[g:g + 1, :]
        outs.append(y.astype(BF16))
    o_ref[0] = jnp.concatenate(outs, axis=1)


def _pool(p3d, w_pool, pool_scale):
    B, S, _ = p3d.shape
    halo_blocks = POOL_TS // POOL_HALO
    return pl.pallas_call(
        _pool_kernel,
        grid=(B, S // POOL_TS),
        in_specs=[
            pl.BlockSpec((1, POOL_TS, POOL_WIDTH), lambda b, i: (b, i, 0)),
            pl.BlockSpec((1, POOL_HALO, POOL_WIDTH),
                         lambda b, i: (b, jnp.maximum(i * halo_blocks - 1, 0), 0)),
            pl.BlockSpec(w_pool.shape, lambda b, i: (0, 0, 0)),
            pl.BlockSpec(pool_scale.shape, lambda b, i: (0, 0)),
        ],
        out_specs=pl.BlockSpec((1, POOL_TS, POOL_WIDTH), lambda b, i: (b, i, 0)),
        out_shape=jax.ShapeDtypeStruct((B, S, POOL_WIDTH), BF16),
        compiler_params=_cparams("arbitrary", "arbitrary"),
        name="pool",
    )(p3d, p3d, w_pool, pool_scale)


ATTN_SPAN = ATTN_BLOCK * max(DILATIONS)
ATTN_GROUP = 8


def _softmax_units(units):
    s = [jnp.where(mask, _dot_nt(q.astype(BF16), k.astype(BF16)), NEG_INF) for q, k, _, mask in units]
    m = [jnp.max(x, axis=1, keepdims=True) for x in s]
    e = [jnp.exp(x - mx) for x, mx in zip(s, m)]
    l = [jnp.sum(x, axis=1, keepdims=True) for x in e]
    o = [_dot(x.astype(BF16), u[2].astype(BF16)) / lx for x, u, lx in zip(e, units, l)]
    return [(ox, mx + jnp.log(lx)) for ox, mx, lx in zip(o, m, l)]


def _attn_kernel(q_ref, kc_ref, kp_ref, vc_ref, vp_ref, o_ref, o_scr, lse_scr):
    has_prev = pl.program_id(2) > 0
    n = ATTN_BLOCK
    rowi = lax.broadcasted_iota(jnp.int32, (n, 2 * n), 0)
    coli = lax.broadcasted_iota(jnp.int32, (n, 2 * n), 1)
    mask = jnp.logical_and(coli >= rowi, coli <= rowi + n)
    mask_first = jnp.logical_and(mask, jnp.logical_or(coli >= n, has_prev))
    todo = []
    for i, d in enumerate(DILATIONS):
        for r in range(d):
            for blk in range(ATTN_SPAN // (n * d)):
                rows = pl.ds(blk * n * d + r, n, stride=d) if d > 1 else pl.ds(blk * n, n)
                if blk > 0:
                    prows = pl.ds((blk - 1) * n * d + r, n, stride=d) if d > 1 else pl.ds((blk - 1) * n, n)
                    todo.append((i, rows, kc_ref, vc_ref, prows, mask))
                else:
                    prows = pl.ds(ATTN_SPAN - n * d + r, n, stride=d) if d > 1 else pl.ds(ATTN_SPAN - n, n)
                    todo.append((i, rows, kp_ref, vp_ref, prows, mask_first))
    for g in range(0, len(todo), ATTN_GROUP):
        group = todo[g:g + ATTN_GROUP]
        units = [(q_ref[0, rows, :],
                  jnp.concatenate([kref[0, prows, :], kc_ref[0, rows, :]], axis=0),
                  jnp.concatenate([vref[0, prows, :], vc_ref[0, rows, :]], axis=0), msk)
                 for _, rows, kref, vref, prows, msk in group]
        for (i, rows, *_), (o, lse) in zip(group, _softmax_units(units)):
            o_scr[i, rows, :] = o
            lse_scr[i, rows, :] = jnp.broadcast_to(lse, o.shape)
    for c in range(ATTN_SPAN // n):
        rows = pl.ds(c * n, n)
        lses = [lse_scr[i, rows, :] for i in range(len(DILATIONS))]
        top = functools.reduce(jnp.maximum, lses)
        ws = [jnp.exp(l - top) for l in lses]
        num = functools.reduce(lambda a, b: a + b, [w * o_scr[i, rows, :] for i, w in enumerate(ws)])
        o_ref[0, rows, :] = num / functools.reduce(lambda a, b: a + b, ws)


def _dilated_attention(q, k, v):
    B, S, W = q.shape
    blk = (1, ATTN_SPAN, HEAD_DIM)
    cur = pl.BlockSpec(blk, lambda b, h, n: (b, n, h))
    prev = pl.BlockSpec(blk, lambda b, h, n: (b, jnp.maximum(n - 1, 0), h))
    state = pltpu.VMEM((len(DILATIONS), ATTN_SPAN, HEAD_DIM), F32)
    return pl.pallas_call(
        _attn_kernel,
        grid=(B, W // HEAD_DIM, S // ATTN_SPAN),
        in_specs=[cur, cur, prev, cur, prev],
        out_specs=cur,
        out_shape=jax.ShapeDtypeStruct((B, S, W), F32),
        scratch_shapes=[state, state],
        compiler_params=_cparams("arbitrary", "arbitrary", "arbitrary"),
        name="attn",
    )(q, k, k, v, v)


def _mem_kv_kernel(mem_ref, g_ref, wk_ref, wv_ref, k_ref, v_ref):
    h = _rms(mem_ref[...], g_ref[...]).astype(BF16)
    k_ref[...] = _dot(h, wk_ref[...]).astype(BF16)
    v_ref[...] = _dot(h, wv_ref[...]).astype(BF16)


def _mem_kv(mem2d, g_mem, w_ck, w_cv):
    M = mem2d.shape[0]
    tm = 256
    full = lambda a: pl.BlockSpec(a.shape, lambda i: (0, 0))
    out_spec = pl.BlockSpec((tm, CROSS_WIDTH), lambda i: (i, 0))
    return pl.pallas_call(
        _mem_kv_kernel,
        grid=(M // tm,),
        in_specs=[pl.BlockSpec((tm, D_MODEL), lambda i: (i, 0)), full(g_mem), full(w_ck), full(w_cv)],
        out_specs=[out_spec, out_spec],
        out_shape=[jax.ShapeDtypeStruct((M, CROSS_WIDTH), BF16)] * 2,
        compiler_params=_cparams("arbitrary"),
        name="mem_kv",
    )(mem2d, g_mem, w_ck, w_cv)


MID_TM = 256


def _mid_kernel(x_ref, pool_ref, attn_ref, wo_ref, gc_ref, wq_ref, mk_ref, mv_ref, wco_ref, gf_ref,
                x2_ref, hft_ref):
    x1 = (x_ref[...] + _dot(pool_ref[...], wo_ref[0:POOL_WIDTH, :])
          + _dot(attn_ref[...].astype(BF16), wo_ref[POOL_WIDTH:, :]))
    hc = _rms(x1, gc_ref[...]).astype(BF16)
    q = (_dot(hc, wq_ref[...]) * (CROSS_HEAD_DIM ** -0.5)).astype(BF16)
    outs = []
    for h in range(CROSS_HEADS):
        sl = slice(h * CROSS_HEAD_DIM, (h + 1) * CROSS_HEAD_DIM)
        s = _dot_nt(q[:, sl], mk_ref[0, :, sl])
        e = jnp.exp(s - jnp.max(s, axis=1, keepdims=True))
        p = e / jnp.sum(e, axis=1, keepdims=True)
        outs.append(_dot(p.astype(BF16), mv_ref[0, :, sl]).astype(BF16))
    o = jnp.concatenate(outs, axis=1)
    x2 = x1 + _dot(o, wco_ref[...])
    x2_ref[...] = x2
    hft_ref[...] = _rms(x2, gf_ref[...]).T.astype(BF16)


def _mid(x2d, pool2d, attn2d, w_out, g_cross, w_cq, mem_k, mem_v, w_co, g_ffn, seq_len):
    T = x2d.shape[0]
    blocks_per_batch = seq_len // MID_TM
    row = lambda i: (i, 0)
    full = lambda a: pl.BlockSpec(a.shape, lambda i: (0,) * a.ndim)
    mem_spec = pl.BlockSpec((1,) + mem_k.shape[1:], lambda i: (i // blocks_per_batch, 0, 0))
    return pl.pallas_call(
        _mid_kernel,
        grid=(T // MID_TM,),
        in_specs=[
            pl.BlockSpec((MID_TM, D_MODEL), row),
            pl.BlockSpec((MID_TM, POOL_WIDTH), row),
            pl.BlockSpec((MID_TM, ATTN_WIDTH), row),
            full(w_out), full(g_cross), full(w_cq), mem_spec, mem_spec, full(w_co), full(g_ffn),
        ],
        out_specs=[pl.BlockSpec((MID_TM, D_MODEL), row), pl.BlockSpec((D_MODEL, MID_TM), lambda i: (0, i))],
        out_shape=[jax.ShapeDtypeStruct((T, D_MODEL), F32), jax.ShapeDtypeStruct((D_MODEL, T), BF16)],
        compiler_params=_cparams("arbitrary"),
        name="mid",
    )(x2d, pool2d, attn2d, w_out, g_cross, w_cq, mem_k, mem_v, w_co, g_ffn)


ROUTER_TB = 256


def _top16(s):
    row = lax.broadcasted_iota(jnp.int32, s.shape, 0)
    vals, idxs = [], []
    for _ in range(PEER_TOPK):
        m = jnp.max(s, axis=0, keepdims=True)
        idx = jnp.min(jnp.where(s == m, row, PEER_KEYS), axis=0, keepdims=True)
        vals.append(m)
        idxs.append(idx)
        s = jnp.where(row == idx, NEG_INF, s)
    return jnp.concatenate(vals, axis=0), jnp.concatenate(idxs, axis=0)


def _router_head(s1, s2):
    n = s1.shape[1]
    v1, i1 = _top16(s1)
    v2, _ = _top16(s2)
    row = lax.broadcasted_iota(jnp.int32, (PEER_TOPK, n), 0)
    top = v1[0:1] + v2[0:1]
    front = v1 + v2[0:1]
    cnt = jnp.zeros((PEER_TOPK, n), jnp.int32)
    zsum = jnp.zeros((1, n), F32)
    for _ in range(PEER_TOPK):
        m = jnp.max(front, axis=0, keepdims=True)
        r = jnp.min(jnp.where(front == m, row, PEER_TOPK), axis=0, keepdims=True)
        pick = row == r
        cnt = jnp.where(pick, cnt + 1, cnt)
        used = jnp.max(jnp.where(pick, cnt, 0), axis=0, keepdims=True)
        nxt = jnp.max(jnp.where(row == used, v2, NEG_INF), axis=0, keepdims=True)
        front = jnp.where(pick, v1 + nxt, front)
        zsum = zsum + jnp.exp(m - top)
    thr_rank = jnp.full((PEER_TOPK, n), POS_INF, F32)
    for j in range(PEER_TOPK):
        thr_rank = jnp.where(cnt == j + 1, v2[j:j + 1], thr_rank)
    key = lax.broadcasted_iota(jnp.int32, (PEER_KEYS, n), 0)
    theta = jnp.full((PEER_KEYS, n), POS_INF, F32)
    for r in range(PEER_TOPK):
        theta = jnp.where(key == i1[r:r + 1], thr_rank[r:r + 1], theta)
    coef = jnp.exp(s1 - v1[0:1]) * (0.5 / zsum)
    e2 = jnp.exp(s2 - v2[0:1])
    return theta, coef, e2


def _router_kernel(hft_ref, wpqt_ref, k1_ref, k2_ref, theta_ref, coef_ref, s2_ref, e2_ref, q_scr):
    q_scr[...] = _dot(wpqt_ref[...], hft_ref[...]).astype(BF16)
    tb = hft_ref.shape[1]
    for h in range(PEER_HEADS):
        base = h * 2 * PEER_HALF
        for c in range(tb // LANES):
            ls = slice(c * LANES, (c + 1) * LANES)
            s1 = _dot(k1_ref[...], q_scr[base:base + PEER_HALF, ls])
            s2 = _dot(k2_ref[...], q_scr[base + PEER_HALF:base + 2 * PEER_HALF, ls])
            theta, coef, e2 = _router_head(s1, s2)
            theta_ref[h, :, ls] = theta
            coef_ref[h, :, ls] = coef
            s2_ref[h, :, ls] = s2
            e2_ref[h, :, ls] = e2


def _router(hft, w_pqt, k1, k2):
    T = hft.shape[1]
    full = lambda a: pl.BlockSpec(a.shape, lambda i: (0, 0))
    out_spec = pl.BlockSpec((PEER_HEADS, PEER_KEYS, ROUTER_TB), lambda i: (0, 0, i))
    return pl.pallas_call(
        _router_kernel,
        grid=(T // ROUTER_TB,),
        in_specs=[pl.BlockSpec((D_MODEL, ROUTER_TB), lambda i: (0, i)), full(w_pqt), full(k1), full(k2)],
        out_specs=[out_spec] * 4,
        out_shape=[jax.ShapeDtypeStruct((PEER_HEADS, PEER_KEYS, T), F32)] * 4,
        scratch_shapes=[pltpu.VMEM((D_MODEL, ROUTER_TB), BF16)],
        compiler_params=_cparams("arbitrary"),
        name="router",
    )(hft, w_pqt, k1, k2)


PEER_TB = 512
PEER_E1 = 8
PEER_TE = PEER_E1 * PEER_KEYS


def _gelu_gate(x, half_gate):
    c0 = math.sqrt(2.0 / math.pi)
    inner = x * (c0 + (c0 * 0.044715) * (x * x))
    return (x * half_gate) * (1.0 + jnp.tanh(inner))


def _peer_kernel(hft_ref, wu_ref, wvt_ref, theta_ref, coef_ref, s2_ref, e2_ref, yt_ref, at_scr, p_scr):
    j = pl.program_id(1)

    @pl.when(j == 0)
    def _():
        yt_ref[...] = jnp.zeros_like(yt_ref)

    at_scr[...] = _dot(wu_ref[...], hft_ref[...])
    tb = hft_ref.shape[1]
    for e in range(PEER_E1):
        rows = slice(e * PEER_KEYS, (e + 1) * PEER_KEYS)
        for c in range(tb // LANES):
            ls = slice(c * LANES, (c + 1) * LANES)
            gate = None
            for h in range(PEER_HEADS):
                hit = s2_ref[h, :, ls] >= theta_ref[h, e:e + 1, ls]
                term = jnp.where(hit, e2_ref[h, :, ls], 0.0) * coef_ref[h, e:e + 1, ls]
                gate = term if gate is None else gate + term
            p_scr[rows, ls] = _gelu_gate(at_scr[rows, ls], gate).astype(BF16)
    yt_ref[...] += _dot(wvt_ref[...], p_scr[...])


def _peer(hft, w_u, w_vt, theta, coef, s2, e2):
    T = hft.shape[1]
    tok = pl.BlockSpec((PEER_HEADS, PEER_KEYS, PEER_TB), lambda i, j: (0, 0, i))
    per_e1 = pl.BlockSpec((PEER_HEADS, PEER_E1, PEER_TB), lambda i, j: (0, j, i))
    return pl.pallas_call(
        _peer_kernel,
        grid=(T // PEER_TB, w_u.shape[0] // PEER_TE),
        in_specs=[
            pl.BlockSpec((D_MODEL, PEER_TB), lambda i, j: (0, i)),
            pl.BlockSpec((PEER_TE, D_MODEL), lambda i, j: (j, 0)),
            pl.BlockSpec((D_MODEL, PEER_TE), lambda i, j: (0, j)),
            per_e1, per_e1, tok, tok,
        ],
        out_specs=pl.BlockSpec((D_MODEL, PEER_TB), lambda i, j: (0, i)),
        out_shape=jax.ShapeDtypeStruct((D_MODEL, T), F32),
        scratch_shapes=[pltpu.VMEM((PEER_TE, PEER_TB), F32), pltpu.VMEM((PEER_TE, PEER_TB), BF16)],
        compiler_params=_cparams("arbitrary", "arbitrary"),
        name="peer",
    )(hft, w_u, w_vt, theta, coef, s2, e2)


FINAL_TM = 512


def _final_kernel(x2_ref, yt_ref, g_ref, o_ref, *, norm):
    x3 = x2_ref[...] + yt_ref[...].T
    o_ref[...] = _rms(x3, g_ref[...]) if norm else x3


def _final(x2, yt, g_final, norm=True):
    T = x2.shape[0]
    return pl.pallas_call(
        functools.partial(_final_kernel, norm=norm),
        grid=(T // FINAL_TM,),
        in_specs=[
            pl.BlockSpec((FINAL_TM, D_MODEL), lambda i: (i, 0)),
            pl.BlockSpec((D_MODEL, FINAL_TM), lambda i: (0, i)),
            pl.BlockSpec((1, D_MODEL), lambda i: (0, 0)),
        ],
        out_specs=pl.BlockSpec((FINAL_TM, D_MODEL), lambda i: (i, 0)),
        out_shape=jax.ShapeDtypeStruct((T, D_MODEL), F32),
        compiler_params=_cparams("arbitrary"),
        name="final",
    )(x2, yt, g_final)


def _layer(x2d, mem2d, pos, inv_lane, B, S, g_mix, w_in, w_pool, pool_scale, w_out, g_cross, g_mem,
           w_cq, w_ck, w_cv, w_co, g_ffn, w_pq, sub_keys_1, sub_keys_2, w_u, w_v):
    row = lambda g: g.reshape(1, -1)
    p, q, k, v = _in_proj(x2d, row(g_mix), pos, inv_lane, w_in.astype(BF16))
    pool = _pool(p.reshape(B, S, POOL_WIDTH), w_pool.astype(BF16), pool_scale)
    to3d = lambda a: a.reshape(B, S, ATTN_WIDTH)
    attn = _dilated_attention(to3d(q), to3d(k), to3d(v))
    mem_k, mem_v = _mem_kv(mem2d, row(g_mem), w_ck.astype(BF16), w_cv.astype(BF16))
    mem_len = mem2d.shape[0] // B
    x2, hft = _mid(x2d, pool.reshape(B * S, POOL_WIDTH), attn.reshape(B * S, ATTN_WIDTH),
                   w_out.astype(BF16), row(g_cross), w_cq.astype(BF16),
                   mem_k.reshape(B, mem_len, CROSS_WIDTH), mem_v.reshape(B, mem_len, CROSS_WIDTH),
                   w_co.astype(BF16), row(g_ffn), S)
    theta, coef, s2, e2 = _router(hft, w_pq.T.astype(BF16), sub_keys_1.astype(BF16),
                                  sub_keys_2.astype(BF16))
    yt = _peer(hft, w_u.astype(BF16), w_v.T.astype(BF16), theta, coef, s2, e2)
    return x2, yt


def kernel(x, mem, positions, g_mix, w_in, w_pool, pool_scale, w_out, g_cross, g_mem, w_cq, w_ck, w_cv,
           w_co, g_ffn, w_pq, sub_keys_1, sub_keys_2, w_u, w_v, g_final):
    B, S, D = x.shape
    depth = w_in.shape[0]
    lane = jnp.arange(LANES)
    inv = ROPE_THETA ** (-(2.0 * (lane % ROPE_HALF)).astype(F32) / ROPE_DIM)
    inv_lane = jnp.where(lane < ROPE_DIM, inv, 0.0).astype(F32).reshape(1, LANES)
    pos = positions.reshape(B * S, 1)
    x2d = x.reshape(B * S, D)
    mem2d = mem.reshape(-1, D)
    for i in range(depth):
        x2, yt = _layer(x2d, mem2d, pos, inv_lane, B, S, g_mix[i], w_in[i], w_pool[i], pool_scale[i],
                        w_out[i], g_cross[i], g_mem[i], w_cq[i], w_ck[i], w_cv[i], w_co[i], g_ffn[i],
                        w_pq[i], sub_keys_1[i], sub_keys_2[i], w_u[i], w_v[i])
        if i + 1 < depth:
            x2d = _final(x2, yt, g_final.reshape(1, D), norm=False)
    return _final(x2, yt, g_final.reshape(1, D)).reshape(B, S, D)
```

```python
import functools
import math

import jax
import jax.numpy as jnp
from jax import lax
from jax.experimental import pallas as pl
from jax.experimental.pallas import tpu as pltpu

D_MODEL = 2048
POOL_WIDTH = 1024
POOL_WINDOWS = (2, 4, 8, 16)
POOL_GROUP = POOL_WIDTH // len(POOL_WINDOWS)
POOL_HALO = 16
ATTN_HEADS = 8
HEAD_DIM = 128
ATTN_WIDTH = ATTN_HEADS * HEAD_DIM
DILATIONS = (1, 4, 16)
ATTN_BLOCK = 128
ROPE_THETA = 500000.0
ROPE_DIM = HEAD_DIM // 4
ROPE_HALF = ROPE_DIM // 2
CROSS_HEADS = 4
CROSS_HEAD_DIM = 128
CROSS_WIDTH = CROSS_HEADS * CROSS_HEAD_DIM
PEER_KEYS = 128
PEER_HEADS = 8
PEER_HALF = 128
PEER_TOPK = 16
EPS = 1e-6

LANES = 128
BF16_SUBLANES = 16
VMEM_LIMIT = 56 * 1024 * 1024

BF16 = jnp.bfloat16
F32 = jnp.float32
NEG_INF = float("-inf")


def _cparams(*sem):
    return pltpu.CompilerParams(dimension_semantics=sem, vmem_limit_bytes=VMEM_LIMIT)


def _rms(xf, g):
    return xf * lax.rsqrt(jnp.mean(xf * xf, axis=-1, keepdims=True) + EPS) * g


def _dot(a, b):
    return jnp.dot(a, b, preferred_element_type=F32)


def _dot_nt(a, b):
    return lax.dot_general(a, b, (((1,), (1,)), ((), ())), preferred_element_type=F32)


IN_TM = 512
IN_SECTIONS = 4


def _in_proj_kernel(x_ref, g_ref, pos_ref, inv_ref, w_ref, p_ref, q_ref, k_ref, v_ref):
    h = _rms(x_ref[...], g_ref[...]).astype(BF16)
    sec = w_ref.shape[1] // IN_SECTIONS
    ang = pos_ref[...].astype(F32) * inv_ref[...]
    cos = jnp.cos(ang)
    sin = jnp.sin(ang)
    lane = lax.broadcasted_iota(jnp.int32, ang.shape, 1)
    sin_hi = jnp.where(lane >= ROPE_HALF, sin, 0.0)
    sin_lo = jnp.where(lane < ROPE_HALF, -sin, 0.0)

    def rope(t):
        return (t * cos + pltpu.roll(t, ROPE_HALF, 1) * sin_hi
                + pltpu.roll(t, HEAD_DIM - ROPE_HALF, 1) * sin_lo)

    p_ref[...] = _dot(h, w_ref[:, 0:sec])
    v_ref[...] = _dot(h, w_ref[:, 3 * sec:4 * sec])
    zq = _dot(h, w_ref[:, sec:2 * sec])
    zk = _dot(h, w_ref[:, 2 * sec:3 * sec])
    for h_idx in range(ATTN_HEADS):
        cols = slice(h_idx * HEAD_DIM, (h_idx + 1) * HEAD_DIM)
        q_ref[:, cols] = rope(zq[:, cols]) * (HEAD_DIM ** -0.5)
        k_ref[:, cols] = rope(zk[:, cols])


def _in_proj(x2d, g, pos, inv_lane, w_in):
    T = x2d.shape[0]
    sec = w_in.shape[1] // IN_SECTIONS
    row = lambda i: (i, 0)
    const = lambda i: (0, 0)
    out_spec = pl.BlockSpec((IN_TM, sec), row)
    return pl.pallas_call(
        _in_proj_kernel,
        grid=(T // IN_TM,),
        in_specs=[
            pl.BlockSpec((IN_TM, D_MODEL), row),
            pl.BlockSpec((1, D_MODEL), const),
            pl.BlockSpec((IN_TM, 1), row),
            pl.BlockSpec((1, LANES), const),
            pl.BlockSpec(w_in.shape, const, pipeline_mode=pl.Buffered(1)),
        ],
        out_specs=[out_spec] * IN_SECTIONS,
        out_shape=[jax.ShapeDtypeStruct((T, sec), F32)] * IN_SECTIONS,
        compiler_params=_cparams("arbitrary"),
        name="in_proj",
    )(x2d, g, pos, inv_lane, w_in)


POOL_TS = 512


def _pool_kernel(cur_ref, halo_ref, w_ref, scale_ref, o_ref):
    si = pl.program_id(1)
    cur = cur_ref[0]
    halo = jnp.where(si > 0, halo_ref[0], 0.0)
    ts = cur.shape[0]
    t = si * ts + lax.broadcasted_iota(jnp.int32, (ts, 1), 0)
    outs = []
    for g, w in enumerate(POOL_WINDOWS):
        lo, hi = g * POOL_GROUP, (g + 1) * POOL_GROUP
        cg = cur[:, lo:hi]
        acc = jnp.concatenate([halo[:, lo:hi], cg], axis=0)
        step = 1
        while step < w:
            acc = acc + pltpu.roll(acc, step, 0)
            step *= 2
        cnt = jnp.minimum(t + 1, w).astype(F32)
        mixed = acc[POOL_HALO:, :] / cnt - cg
        y = _dot(mixed.astype(BF16), w_ref[g]) * scale_ref[g:g + 1, :]
        outs.append(y.astype(BF16))
    o_ref[0] = jnp.concatenate(outs, axis=1)


def _pool(p3d, w_pool, pool_scale):
    B, S, _ = p3d.shape
    halo_blocks = POOL_TS // POOL_HALO
    return pl.pallas_call(
        _pool_kernel,
        grid=(B, S // POOL_TS),
        in_specs=[
            pl.BlockSpec((1, POOL_TS, POOL_WIDTH), lambda b, i: (b, i, 0)),
            pl.BlockSpec((1, POOL_HALO, POOL_WIDTH),
                         lambda b, i: (b, jnp.maximum(i * halo_blocks - 1, 0), 0)),
            pl.BlockSpec(w_pool.shape, lambda b, i: (0, 0, 0)),
            pl.BlockSpec(pool_scale.shape, lambda b, i: (0, 0)),
        ],
        out_specs=pl.BlockSpec((1, POOL_TS, POOL_WIDTH), lambda b, i: (b, i, 0)),
        out_shape=jax.ShapeDtypeStruct((B, S, POOL_WIDTH), BF16),
        compiler_params=_cparams("arbitrary", "arbitrary"),
        name="pool",
    )(p3d, p3d, w_pool, pool_scale)


ATTN_SPAN = ATTN_BLOCK * max(DILATIONS)
ATTN_GROUP = 8


def _softmax_units(units):
    s = [jnp.where(mask, _dot_nt(q.astype(BF16), k.astype(BF16)), NEG_INF) for q, k, _, mask in units]
    m = [jnp.max(x, axis=1, keepdims=True) for x in s]
    e = [jnp.exp(x - mx) for x, mx in zip(s, m)]
    l = [jnp.sum(x, axis=1, keepdims=True) for x in e]
    o = [_dot(x.astype(BF16), u[2].astype(BF16)) / lx for x, u, lx in zip(e, units, l)]
    return [(ox, mx + jnp.log(lx)) for ox, mx, lx in zip(o, m, l)]


def _attn_kernel(q_ref, kc_ref, kp_ref, vc_ref, vp_ref, o_ref, o_scr, lse_scr):
    has_prev = pl.program_id(2) > 0
    n = ATTN_BLOCK
    rowi = lax.broadcasted_iota(jnp.int32, (n, 2 * n), 0)
    coli = lax.broadcasted_iota(jnp.int32, (n, 2 * n), 1)
    mask = jnp.logical_and(coli >= rowi, coli <= rowi + n)
    mask_first = jnp.logical_and(mask, jnp.logical_or(coli >= n, has_prev))
    todo = []
    for i, d in enumerate(DILATIONS):
        for r in range(d):
            for blk in range(ATTN_SPAN // (n * d)):
                rows = pl.ds(blk * n * d + r, n, stride=d) if d > 1 else pl.ds(blk * n, n)
                if blk > 0:
                    prows = pl.ds((blk - 1) * n * d + r, n, stride=d) if d > 1 else pl.ds((blk - 1) * n, n)
                    todo.append((i, rows, kc_ref, vc_ref, prows, mask))
                else:
                    prows = pl.ds(ATTN_SPAN - n * d + r, n, stride=d) if d > 1 else pl.ds(ATTN_SPAN - n, n)
                    todo.append((i, rows, kp_ref, vp_ref, prows, mask_first))
    for g in range(0, len(todo), ATTN_GROUP):
        group = todo[g:g + ATTN_GROUP]
        units = [(q_ref[0, rows, :],
                  jnp.concatenate([kref[0, prows, :], kc_ref[0, rows, :]], axis=0),
                  jnp.concatenate([vref[0, prows, :], vc_ref[0, rows, :]], axis=0), msk)
                 for _, rows, kref, vref, prows, msk in group]
        for (i, rows, *_), (o, lse) in zip(group, _softmax_units(units)):
            o_scr[i, rows, :] = o
            lse_scr[i, rows, :] = jnp.broadcast_to(lse, o.shape)
    for c in range(ATTN_SPAN // n):
        rows = pl.ds(c * n, n)
        lses = [lse_scr[i, rows, :] for i in range(len(DILATIONS))]
        top = functools.reduce(jnp.maximum, lses)
        ws = [jnp.exp(l - top) for l in lses]
        num = functools.reduce(lambda a, b: a + b, [w * o_scr[i, rows, :] for i, w in enumerate(ws)])
        o_ref[0, rows, :] = num / functools.reduce(lambda a, b: a + b, ws)


def _dilated_attention(q, k, v):
    B, S, W = q.shape
    blk = (1, ATTN_SPAN, HEAD_DIM)
    cur = pl.BlockSpec(blk, lambda b, h, n: (b, n, h))
    prev = pl.BlockSpec(blk, lambda b, h, n: (b, jnp.maximum(n - 1, 0), h))
    state = pltpu.VMEM((len(DILATIONS), ATTN_SPAN, HEAD_DIM), F32)
    return pl.pallas_call(
        _attn_kernel,
        grid=(B, W // HEAD_DIM, S // ATTN_SPAN),
        in_specs=[cur, cur, prev, cur, prev],
        out_specs=cur,
        out_shape=jax.ShapeDtypeStruct((B, S, W), F32),
        scratch_shapes=[state, state],
        compiler_params=_cparams("arbitrary", "arbitrary", "arbitrary"),
        name="attn",
    )(q, k, k, v, v)


def _mem_kv_kernel(mem_ref, g_ref, wk_ref, wv_ref, k_ref, v_ref):
    h = _rms(mem_ref[...], g_ref[...]).astype(BF16)
    k_ref[...] = _dot(h, wk_ref[...]).astype(BF16)
    v_ref[...] = _dot(h, wv_ref[...]).astype(BF16)


def _mem_kv(mem2d, g_mem, w_ck, w_cv):
    M = mem2d.shape[0]
    tm = 256
    full = lambda a: pl.BlockSpec(a.shape, lambda i: (0, 0))
    out_spec = pl.BlockSpec((tm, CROSS_WIDTH), lambda i: (i, 0))
    return pl.pallas_call(
        _mem_kv_kernel,
        grid=(M // tm,),
        in_specs=[pl.BlockSpec((tm, D_MODEL), lambda i: (i, 0)), full(g_mem), full(w_ck), full(w_cv)],
        out_specs=[out_spec, out_spec],
        out_shape=[jax.ShapeDtypeStruct((M, CROSS_WIDTH), BF16)] * 2,
        compiler_params=_cparams("arbitrary"),
        name="mem_kv",
    )(mem2d, g_mem, w_ck, w_cv)


MID_TM = 256


def _mid_kernel(x_ref, pool_ref, attn_ref, wo_ref, gc_ref, wq_ref, mk_ref, mv_ref, wco_ref, gf_ref,
                x2_ref, hft_ref):
    x1 = (x_ref[...] + _dot(pool_ref[...], wo_ref[0:POOL_WIDTH, :])
          + _dot(attn_ref[...].astype(BF16), wo_ref[POOL_WIDTH:, :]))
    hc = _rms(x1, gc_ref[...]).astype(BF16)
    q = (_dot(hc, wq_ref[...]) * (CROSS_HEAD_DIM ** -0.5)).astype(BF16)
    outs = []
    for h in range(CROSS_HEADS):
        sl = slice(h * CROSS_HEAD_DIM, (h + 1) * CROSS_HEAD_DIM)
        s = _dot_nt(q[:, sl], mk_ref[0, :, sl])
        e = jnp.exp(s - jnp.max(s, axis=1, keepdims=True))
        p = e / jnp.sum(e, axis=1, keepdims=True)
        outs.append(_dot(p.astype(BF16), mv_ref[0, :, sl]).astype(BF16))
    o = jnp.concatenate(outs, axis=1)
    x2 = x1 + _dot(o, wco_ref[...])
    x2_ref[...] = x2
    hft_ref[...] = _rms(x2, gf_ref[...]).T.astype(BF16)


def _mid(x2d, pool2d, attn2d, w_out, g_cross, w_cq, mem_k, mem_v, w_co, g_ffn, seq_len):
    T = x2d.shape[0]
    blocks_per_batch = seq_len // MID_TM
    row = lambda i: (i, 0)
    full = lambda a: pl.BlockSpec(a.shape, lambda i: (0,) * a.ndim)
    mem_spec = pl.BlockSpec((1,) + mem_k.shape[1:], lambda i: (i // blocks_per_batch, 0, 0))
    return pl.pallas_call(
        _mid_kernel,
        grid=(T // MID_TM,),
        in_specs=[
            pl.BlockSpec((MID_TM, D_MODEL), row),
            pl.BlockSpec((MID_TM, POOL_WIDTH), row),
            pl.BlockSpec((MID_TM, ATTN_WIDTH), row),
            full(w_out), full(g_cross), full(w_cq), mem_spec, mem_spec, full(w_co), full(g_ffn),
        ],
        out_specs=[pl.BlockSpec((MID_TM, D_MODEL), row), pl.BlockSpec((D_MODEL, MID_TM), lambda i: (0, i))],
        out_shape=[jax.ShapeDtypeStruct((T, D_MODEL), F32), jax.ShapeDtypeStruct((D_MODEL, T), BF16)],
        compiler_params=_cparams("arbitrary"),
        name="mid",
    )(x2d, pool2d, attn2d, w_out, g_cross, w_cq, mem_k, mem_v, w_co, g_ffn)


ROUTER_TB = 256


def _top16(scores):
    row = lax.broadcasted_iota(jnp.int32, scores[0].shape, 0)
    vals = [[] for _ in scores]
    idxs = [[] for _ in scores]
    for _ in range(PEER_TOPK):
        m = [jnp.max(s, axis=0, keepdims=True) for s in scores]
        idx = [jnp.min(jnp.where(s == mx, row, PEER_KEYS), axis=0, keepdims=True)
               for s, mx in zip(scores, m)]
        scores = [jnp.where(row == ix, NEG_INF, s) for s, ix in zip(scores, idx)]
        for k in range(len(scores)):
            vals[k].append(m[k])
            idxs[k].append(idx[k])
    return [(jnp.concatenate(v, axis=0), jnp.concatenate(i, axis=0)) for v, i in zip(vals, idxs)]


def _router_head(s1, s2):
    n = s1.shape[1]
    (v1, i1), (v2, i2) = _top16([s1, s2])
    row = lax.broadcasted_iota(jnp.int32, (PEER_TOPK, n), 0)
    top = v1[0:1] + v2[0:1]
    front = v1 + v2[0:1]
    cnt = jnp.zeros((PEER_TOPK, n), jnp.int32)
    zsum = jnp.zeros((1, n), F32)
    for _ in range(PEER_TOPK):
        m = jnp.max(front, axis=0, keepdims=True)
        r = jnp.min(jnp.where(front == m, row, PEER_TOPK), axis=0, keepdims=True)
        pick = row == r
        cnt = jnp.where(pick, cnt + 1, cnt)
        used = jnp.max(jnp.where(pick, cnt, 0), axis=0, keepdims=True)
        nxt = jnp.max(jnp.where(row == used, v2, NEG_INF), axis=0, keepdims=True)
        front = jnp.where(pick, v1 + nxt, front)
        zsum = zsum + jnp.exp(m - top)
    key = lax.broadcasted_iota(jnp.int32, (PEER_KEYS, n), 0)
    count = jnp.zeros((PEER_KEYS, n), jnp.int32)
    rank2 = jnp.full((PEER_KEYS, n), PEER_KEYS, jnp.int32)
    for r in range(PEER_TOPK):
        count = jnp.where(key == i1[r:r + 1], cnt[r:r + 1], count)
        rank2 = jnp.where(key == i2[r:r + 1], r, rank2)
    coef = jnp.exp(s1 - v1[0:1]) * (0.5 / zsum)
    e2 = jnp.exp(s2 - v2[0:1])
    return count.astype(F32), coef, rank2.astype(F32), e2


def _router_kernel(hft_ref, wpqt_ref, k1_ref, k2_ref, count_ref, coef_ref, rank2_ref, e2_ref, q_scr):
    q_scr[...] = _dot(wpqt_ref[...], hft_ref[...]).astype(BF16)
    tb = hft_ref.shape[1]
    for h in range(PEER_HEADS):
        base = h * 2 * PEER_HALF
        for c in range(tb // LANES):
            ls = slice(c * LANES, (c + 1) * LANES)
            s1 = _dot(k1_ref[...], q_scr[base:base + PEER_HALF, ls])
            s2 = _dot(k2_ref[...], q_scr[base + PEER_HALF:base + 2 * PEER_HALF, ls])
            count, coef, rank2, e2 = _router_head(s1, s2)
            count_ref[h, :, ls] = count
            coef_ref[h, :, ls] = coef
            rank2_ref[h, :, ls] = rank2.astype(BF16)
            e2_ref[h, :, ls] = e2.astype(BF16)


def _router(hft, w_pqt, k1, k2):
    T = hft.shape[1]
    full = lambda a: pl.BlockSpec(a.shape, lambda i: (0, 0))
    out_spec = pl.BlockSpec((PEER_HEADS, PEER_KEYS, ROUTER_TB), lambda i: (0, 0, i))
    return pl.pallas_call(
        _router_kernel,
        grid=(T // ROUTER_TB,),
        in_specs=[pl.BlockSpec((D_MODEL, ROUTER_TB), lambda i: (0, i)), full(w_pqt), full(k1), full(k2)],
        out_specs=[out_spec] * 4,
        out_shape=[jax.ShapeDtypeStruct((PEER_HEADS, PEER_KEYS, T), dt) for dt in (F32, F32, BF16, BF16)],
        scratch_shapes=[pltpu.VMEM((D_MODEL, ROUTER_TB), BF16)],
        compiler_params=_cparams("arbitrary"),
        name="router",
    )(hft, w_pqt, k1, k2)


PEER_TB = 1024
PEER_E1 = 8
PEER_TE = PEER_E1 * PEER_KEYS


def _gelu_gate(x, half_gate):
    c0 = math.sqrt(2.0 / math.pi)
    inner = x * (c0 + (c0 * 0.044715) * (x * x))
    return (x * half_gate) * (1.0 + jnp.tanh(inner))


def _peer_kernel(hft_ref, wu_lo_ref, wu_hi_ref, wvt_lo_ref, wvt_hi_ref, count_ref, coef_ref, rank2_ref,
                 e2_ref, yt_ref, at_scr, p_scr):
    j = pl.program_id(1)

    @pl.when(j == 0)
    def _():
        yt_ref[...] = jnp.zeros_like(yt_ref)

    half_e = PEER_TE // 2
    at_scr[0:half_e, :] = _dot(wu_lo_ref[...], hft_ref[...])
    at_scr[half_e:, :] = _dot(wu_hi_ref[...], hft_ref[...])
    tb = hft_ref.shape[1]
    pack = BF16_SUBLANES
    groups = (PEER_KEYS // pack, pack, LANES)
    for e in range(PEER_E1):
        rows = slice(e * PEER_KEYS, (e + 1) * PEER_KEYS)
        cnt = [jnp.broadcast_to(count_ref[h, e:e + 1, :], (pack, tb)).astype(BF16) for h in range(PEER_HEADS)]
        cf = [jnp.broadcast_to(coef_ref[h, e:e + 1, :], (pack, tb)).astype(BF16) for h in range(PEER_HEADS)]
        for c in range(tb // LANES):
            ls = slice(c * LANES, (c + 1) * LANES)
            gate = None
            for h in range(PEER_HEADS):
                hit = rank2_ref[h, :, ls].reshape(groups) < cnt[h][None, :, ls]
                term = jnp.where(hit, e2_ref[h, :, ls].reshape(groups), 0.0) * cf[h][None, :, ls]
                gate = term if gate is None else gate + term
            gate = gate.astype(F32).reshape(PEER_KEYS, LANES)
            p_scr[rows, ls] = _gelu_gate(at_scr[rows, ls], gate).astype(BF16)
    half_d = D_MODEL // 2
    yt_ref[0:half_d, :] += _dot(wvt_lo_ref[...], p_scr[...])
    yt_ref[half_d:, :] += _dot(wvt_hi_ref[...], p_scr[...])


def _peer(hft, w_u, w_vt, count, coef, rank2, e2):
    T = hft.shape[1]
    once = pl.Buffered(1)
    tok = pl.BlockSpec((PEER_HEADS, PEER_KEYS, PEER_TB), lambda i, j: (0, 0, i), pipeline_mode=once)
    per_e1 = pl.BlockSpec((PEER_HEADS, PEER_E1, PEER_TB), lambda i, j: (0, j, i))
    return pl.pallas_call(
        _peer_kernel,
        grid=(T // PEER_TB, w_u.shape[0] // PEER_TE),
        in_specs=[
            pl.BlockSpec((D_MODEL, PEER_TB), lambda i, j: (0, i), pipeline_mode=once),
            pl.BlockSpec((PEER_TE // 2, D_MODEL), lambda i, j: (2 * j, 0)),
            pl.BlockSpec((PEER_TE // 2, D_MODEL), lambda i, j: (2 * j + 1, 0)),
            pl.BlockSpec((D_MODEL // 2, PEER_TE), lambda i, j: (0, j)),
            pl.BlockSpec((D_MODEL // 2, PEER_TE), lambda i, j: (1, j)),
            per_e1, per_e1, tok, tok,
        ],
        out_specs=pl.BlockSpec((D_MODEL, PEER_TB), lambda i, j: (0, i)),
        out_shape=jax.ShapeDtypeStruct((D_MODEL, T), F32),
        scratch_shapes=[pltpu.VMEM((PEER_TE, PEER_TB), F32), pltpu.VMEM((PEER_TE, PEER_TB), BF16)],
        compiler_params=_cparams("arbitrary", "arbitrary"),
        name="peer",
    )(hft, w_u, w_u, w_vt, w_vt, count, coef, rank2, e2)


FINAL_TM = 512


def _final_kernel(x2_ref, yt_ref, g_ref, o_ref, *, norm):
    x3 = x2_ref[...] + yt_ref[...].T
    o_ref[...] = _rms(x3, g_ref[...]) if norm else x3


def _final(x2, yt, g_final, norm=True):
    T = x2.shape[0]
    return pl.pallas_call(
        functools.partial(_final_kernel, norm=norm),
        grid=(T // FINAL_TM,),
        in_specs=[
            pl.BlockSpec((FINAL_TM, D_MODEL), lambda i: (i, 0)),
            pl.BlockSpec((D_MODEL, FINAL_TM), lambda i: (0, i)),
            pl.BlockSpec((1, D_MODEL), lambda i: (0, 0)),
        ],
        out_specs=pl.BlockSpec((FINAL_TM, D_MODEL), lambda i: (i, 0)),
        out_shape=jax.ShapeDtypeStruct((T, D_MODEL), F32),
        compiler_params=_cparams("arbitrary"),
        name="final",
    )(x2, yt, g_final)


def _layer(x2d, mem2d, pos, inv_lane, B, S, g_mix, w_in, w_pool, pool_scale, w_out, g_cross, g_mem,
           w_cq, w_ck, w_cv, w_co, g_ffn, w_pq, sub_keys_1, sub_keys_2, w_u, w_v):
    row = lambda g: g.reshape(1, -1)
    p, q, k, v = _in_proj(x2d, row(g_mix), pos, inv_lane, w_in.astype(BF16))
    pool = _pool(p.reshape(B, S, POOL_WIDTH), w_pool.astype(BF16), pool_scale)
    to3d = lambda a: a.reshape(B, S, ATTN_WIDTH)
    attn = _dilated_attention(to3d(q), to3d(k), to3d(v))
    mem_k, mem_v = _mem_kv(mem2d, row(g_mem), w_ck.astype(BF16), w_cv.astype(BF16))
    mem_len = mem2d.shape[0] // B
    x2, hft = _mid(x2d, pool.reshape(B * S, POOL_WIDTH), attn.reshape(B * S, ATTN_WIDTH),
                   w_out.astype(BF16), row(g_cross), w_cq.astype(BF16),
                   mem_k.reshape(B, mem_len, CROSS_WIDTH), mem_v.reshape(B, mem_len, CROSS_WIDTH),
                   w_co.astype(BF16), row(g_ffn), S)
    count, coef, rank2, e2 = _router(hft, w_pq.T.astype(BF16), sub_keys_1.astype(BF16),
                                     sub_keys_2.astype(BF16))
    yt = _peer(hft, w_u.astype(BF16), w_v.T.astype(BF16), count, coef, rank2, e2)
    return x2, yt


def kernel(x, mem, positions, g_mix, w_in, w_pool, pool_scale, w_out, g_cross, g_mem, w_cq, w_ck, w_cv,
           w_co, g_ffn, w_pq, sub_keys_1, sub_keys_2, w_u, w_v, g_final):
    B, S, D = x.shape
    depth = w_in.shape[0]
    lane = jnp.arange(LANES)
    inv = ROPE_THETA ** (-(2.0 * (lane % ROPE_HALF)).astype(F32) / ROPE_DIM)
    inv_lane = jnp.where(lane < ROPE_DIM, inv, 0.0).astype(F32).reshape(1, LANES)
    pos = positions.reshape(B * S, 1)
    x2d = x.reshape(B * S, D)
    mem2d = mem.reshape(-1, D)
    for i in range(depth):
        x2, yt = _layer(x2d, mem2d, pos, inv_lane, B, S, g_mix[i], w_in[i], w_pool[i], pool_scale[i],
                        w_out[i], g_cross[i], g_mem[i], w_cq[i], w_ck[i], w_cv[i], w_co[i], g_ffn[i],
                        w_pq[i], sub_keys_1[i], sub_keys_2[i], w_u[i], w_v[i])
        if i + 1 < depth:
            x2d = _final(x2, yt, g_final.reshape(1, D), norm=False)
    return _final(x2, yt, g_final.reshape(1, D)).reshape(B, S, D)
```

```python
import functools
import math

import jax
import jax.numpy as jnp
from jax import lax
from jax.experimental import pallas as pl
from jax.experimental.pallas import tpu as pltpu

D_MODEL = 2048
POOL_WIDTH = 1024
POOL_WINDOWS = (2, 4, 8, 16)
POOL_GROUP = POOL_WIDTH // len(POOL_WINDOWS)
POOL_HALO = 16
ATTN_HEADS = 8
HEAD_DIM = 128
ATTN_WIDTH = ATTN_HEADS * HEAD_DIM
DILATIONS = (1, 4, 16)
ATTN_BLOCK = 128
ROPE_THETA = 500000.0
ROPE_DIM = HEAD_DIM // 4
ROPE_HALF = ROPE_DIM // 2
CROSS_HEADS = 4
CROSS_HEAD_DIM = 128
CROSS_WIDTH = CROSS_HEADS * CROSS_HEAD_DIM
PEER_KEYS = 128
PEER_HEADS = 8
PEER_HALF = 128
PEER_TOPK = 16
EPS = 1e-6

LANES = 128
BF16_SUBLANES = 16
VMEM_LIMIT = 56 * 1024 * 1024

BF16 = jnp.bfloat16
F32 = jnp.float32
NEG_INF = float("-inf")


def _cparams(*sem):
    return pltpu.CompilerParams(dimension_semantics=sem, vmem_limit_bytes=VMEM_LIMIT)


def _rms(xf, g):
    return xf * lax.rsqrt(jnp.mean(xf * xf, axis=-1, keepdims=True) + EPS) * g


def _dot(a, b):
    return jnp.dot(a, b, preferred_element_type=F32)


def _dot_nt(a, b):
    return lax.dot_general(a, b, (((1,), (1,)), ((), ())), preferred_element_type=F32)


IN_TM = 512
IN_SECTIONS = 4


def _in_proj_kernel(x_ref, g_ref, pos_ref, inv_ref, w_ref, p_ref, q_ref, k_ref, v_ref):
    h = _rms(x_ref[...], g_ref[...]).astype(BF16)
    sec = w_ref.shape[1] // IN_SECTIONS
    ang = pos_ref[...].astype(F32) * inv_ref[...]
    cos = jnp.cos(ang)
    sin = jnp.sin(ang)
    lane = lax.broadcasted_iota(jnp.int32, ang.shape, 1)
    sin_hi = jnp.where(lane >= ROPE_HALF, sin, 0.0)
    sin_lo = jnp.where(lane < ROPE_HALF, -sin, 0.0)

    def rope(t):
        return (t * cos + pltpu.roll(t, ROPE_HALF, 1) * sin_hi
                + pltpu.roll(t, HEAD_DIM - ROPE_HALF, 1) * sin_lo)

    p_ref[...] = _dot(h, w_ref[:, 0:sec])
    v_ref[...] = _dot(h, w_ref[:, 3 * sec:4 * sec])
    zq = _dot(h, w_ref[:, sec:2 * sec])
    zk = _dot(h, w_ref[:, 2 * sec:3 * sec])
    for h_idx in range(ATTN_HEADS):
        cols = slice(h_idx * HEAD_DIM, (h_idx + 1) * HEAD_DIM)
        q_ref[:, cols] = rope(zq[:, cols]) * (HEAD_DIM ** -0.5)
        k_ref[:, cols] = rope(zk[:, cols])


def _in_proj(x2d, g, pos, inv_lane, w_in):
    T = x2d.shape[0]
    sec = w_in.shape[1] // IN_SECTIONS
    row = lambda i: (i, 0)
    const = lambda i: (0, 0)
    out_spec = pl.BlockSpec((IN_TM, sec), row)
    return pl.pallas_call(
        _in_proj_kernel,
        grid=(T // IN_TM,),
        in_specs=[
            pl.BlockSpec((IN_TM, D_MODEL), row),
            pl.BlockSpec((1, D_MODEL), const),
            pl.BlockSpec((IN_TM, 1), row),
            pl.BlockSpec((1, LANES), const),
            pl.BlockSpec(w_in.shape, const, pipeline_mode=pl.Buffered(1)),
        ],
        out_specs=[out_spec] * IN_SECTIONS,
        out_shape=[jax.ShapeDtypeStruct((T, sec), F32)] * IN_SECTIONS,
        compiler_params=_cparams("arbitrary"),
        name="in_proj",
    )(x2d, g, pos, inv_lane, w_in)


POOL_TS = 512


def _pool_kernel(cur_ref, halo_ref, w_ref, scale_ref, o_ref):
    si = pl.program_id(1)
    cur = cur_ref[0]
    halo = jnp.where(si > 0, halo_ref[0], 0.0)
    ts = cur.shape[0]
    t = si * ts + lax.broadcasted_iota(jnp.int32, (ts, 1), 0)
    outs = []
    for g, w in enumerate(POOL_WINDOWS):
        lo, hi = g * POOL_GROUP, (g + 1) * POOL_GROUP
        cg = cur[:, lo:hi]
        acc = jnp.concatenate([halo[:, lo:hi], cg], axis=0)
        step = 1
        while step < w:
            acc = acc + pltpu.roll(acc, step, 0)
            step *= 2
        cnt = jnp.minimum(t + 1, w).astype(F32)
        mixed = acc[POOL_HALO:, :] / cnt - cg
        y = _dot(mixed.astype(BF16), w_ref[g]) * scale_ref[g:g + 1, :]
        outs.append(y.astype(BF16))
    o_ref[0] = jnp.concatenate(outs, axis=1)


def _pool(p3d, w_pool, pool_scale):
    B, S, _ = p3d.shape
    halo_blocks = POOL_TS // POOL_HALO
    return pl.pallas_call(
        _pool_kernel,
        grid=(B, S // POOL_TS),
        in_specs=[
            pl.BlockSpec((1, POOL_TS, POOL_WIDTH), lambda b, i: (b, i, 0)),
            pl.BlockSpec((1, POOL_HALO, POOL_WIDTH),
                         lambda b, i: (b, jnp.maximum(i * halo_blocks - 1, 0), 0)),
            pl.BlockSpec(w_pool.shape, lambda b, i: (0, 0, 0)),
            pl.BlockSpec(pool_scale.shape, lambda b, i: (0, 0)),
        ],
        out_specs=pl.BlockSpec((1, POOL_TS, POOL_WIDTH), lambda b, i: (b, i, 0)),
        out_shape=jax.ShapeDtypeStruct((B, S, POOL_WIDTH), BF16),
        compiler_params=_cparams("arbitrary", "arbitrary"),
        name="pool",
    )(p3d, p3d, w_pool, pool_scale)


ATTN_SPAN = ATTN_BLOCK * max(DILATIONS)
ATTN_GROUP = 8


def _softmax_units(units):
    s = [jnp.where(mask, _dot_nt(q.astype(BF16), k.astype(BF16)), NEG_INF) for q, k, _, mask in units]
    m = [jnp.max(x, axis=1, keepdims=True) for x in s]
    e = [jnp.exp(x - mx) for x, mx in zip(s, m)]
    l = [jnp.sum(x, axis=1, keepdims=True) for x in e]
    o = [_dot(x.astype(BF16), u[2].astype(BF16)) / lx for x, u, lx in zip(e, units, l)]
    return [(ox, mx + jnp.log(lx)) for ox, mx, lx in zip(o, m, l)]


def _attn_kernel(q_ref, kc_ref, kp_ref, vc_ref, vp_ref, o_ref, o_scr, lse_scr):
    has_prev = pl.program_id(2) > 0
    n = ATTN_BLOCK
    rowi = lax.broadcasted_iota(jnp.int32, (n, 2 * n), 0)
    coli = lax.broadcasted_iota(jnp.int32, (n, 2 * n), 1)
    mask = jnp.logical_and(coli >= rowi, coli <= rowi + n)
    mask_first = jnp.logical_and(mask, jnp.logical_or(coli >= n, has_prev))
    todo = []
    for i, d in enumerate(DILATIONS):
        for r in range(d):
            for blk in range(ATTN_SPAN // (n * d)):
                rows = pl.ds(blk * n * d + r, n, stride=d) if d > 1 else pl.ds(blk * n, n)
                if blk > 0:
                    prows = pl.ds((blk - 1) * n * d + r, n, stride=d) if d > 1 else pl.ds((blk - 1) * n, n)
                    todo.append((i, rows, kc_ref, vc_ref, prows, mask))
                else:
                    prows = pl.ds(ATTN_SPAN - n * d + r, n, stride=d) if d > 1 else pl.ds(ATTN_SPAN - n, n)
                    todo.append((i, rows, kp_ref, vp_ref, prows, mask_first))
    for g in range(0, len(todo), ATTN_GROUP):
        group = todo[g:g + ATTN_GROUP]
        units = [(q_ref[0, rows, :],
                  jnp.concatenate([kref[0, prows, :], kc_ref[0, rows, :]], axis=0),
                  jnp.concatenate([vref[0, prows, :], vc_ref[0, rows, :]], axis=0), msk)
                 for _, rows, kref, vref, prows, msk in group]
        for (i, rows, *_), (o, lse) in zip(group, _softmax_units(units)):
            o_scr[i, rows, :] = o
            lse_scr[i, rows, :] = jnp.broadcast_to(lse, o.shape)
    for c in range(ATTN_SPAN // n):
        rows = pl.ds(c * n, n)
        lses = [lse_scr[i, rows, :] for i in range(len(DILATIONS))]
        top = functools.reduce(jnp.maximum, lses)
        ws = [jnp.exp(l - top) for l in lses]
        num = functools.reduce(lambda a, b: a + b, [w * o_scr[i, rows, :] for i, w in enumerate(ws)])
        o_ref[0, rows, :] = num / functools.reduce(lambda a, b: a + b, ws)


def _dilated_attention(q, k, v):
    B, S, W = q.shape
    blk = (1, ATTN_SPAN, HEAD_DIM)
    cur = pl.BlockSpec(blk, lambda b, h, n: (b, n, h))
    prev = pl.BlockSpec(blk, lambda b, h, n: (b, jnp.maximum(n - 1, 0), h))
    state = pltpu.VMEM((len(DILATIONS), ATTN_SPAN, HEAD_DIM), F32)
    return pl.pallas_call(
        _attn_kernel,
        grid=(B, W // HEAD_DIM, S // ATTN_SPAN),
        in_specs=[cur, cur, prev, cur, prev],
        out_specs=cur,
        out_shape=jax.ShapeDtypeStruct((B, S, W), F32),
        scratch_shapes=[state, state],
        compiler_params=_cparams("arbitrary", "arbitrary", "arbitrary"),
        name="attn",
    )(q, k, k, v, v)


def _mem_kv_kernel(mem_ref, g_ref, wk_ref, wv_ref, k_ref, v_ref):
    h = _rms(mem_ref[...], g_ref[...]).astype(BF16)
    k_ref[...] = _dot(h, wk_ref[...]).astype(BF16)
    v_ref[...] = _dot(h, wv_ref[...]).astype(BF16)


def _mem_kv(mem2d, g_mem, w_ck, w_cv):
    M = mem2d.shape[0]
    tm = 256
    full = lambda a: pl.BlockSpec(a.shape, lambda i: (0, 0))
    out_spec = pl.BlockSpec((tm, CROSS_WIDTH), lambda i: (i, 0))
    return pl.pallas_call(
        _mem_kv_kernel,
        grid=(M // tm,),
        in_specs=[pl.BlockSpec((tm, D_MODEL), lambda i: (i, 0)), full(g_mem), full(w_ck), full(w_cv)],
        out_specs=[out_spec, out_spec],
        out_shape=[jax.ShapeDtypeStruct((M, CROSS_WIDTH), BF16)] * 2,
        compiler_params=_cparams("arbitrary"),
        name="mem_kv",
    )(mem2d, g_mem, w_ck, w_cv)


MID_TM = 256


def _mid_kernel(x_ref, pool_ref, attn_ref, wo_ref, gc_ref, wq_ref, mk_ref, mv_ref, wco_ref, gf_ref,
                x2_ref, hft_ref):
    x1 = (x_ref[...] + _dot(pool_ref[...], wo_ref[0:POOL_WIDTH, :])
          + _dot(attn_ref[...].astype(BF16), wo_ref[POOL_WIDTH:, :]))
    hc = _rms(x1, gc_ref[...]).astype(BF16)
    q = (_dot(hc, wq_ref[...]) * (CROSS_HEAD_DIM ** -0.5)).astype(BF16)
    outs = []
    for h in range(CROSS_HEADS):
        sl = slice(h * CROSS_HEAD_DIM, (h + 1) * CROSS_HEAD_DIM)
        s = _dot_nt(q[:, sl], mk_ref[0, :, sl])
        e = jnp.exp(s - jnp.max(s, axis=1, keepdims=True))
        p = e / jnp.sum(e, axis=1, keepdims=True)
        outs.append(_dot(p.astype(BF16), mv_ref[0, :, sl]).astype(BF16))
    o = jnp.concatenate(outs, axis=1)
    x2 = x1 + _dot(o, wco_ref[...])
    x2_ref[...] = x2
    hft_ref[...] = _rms(x2, gf_ref[...]).T.astype(BF16)


def _mid(x2d, pool2d, attn2d, w_out, g_cross, w_cq, mem_k, mem_v, w_co, g_ffn, seq_len):
    T = x2d.shape[0]
    blocks_per_batch = seq_len // MID_TM
    row = lambda i: (i, 0)
    full = lambda a: pl.BlockSpec(a.shape, lambda i: (0,) * a.ndim)
    mem_spec = pl.BlockSpec((1,) + mem_k.shape[1:], lambda i: (i // blocks_per_batch, 0, 0))
    return pl.pallas_call(
        _mid_kernel,
        grid=(T // MID_TM,),
        in_specs=[
            pl.BlockSpec((MID_TM, D_MODEL), row),
            pl.BlockSpec((MID_TM, POOL_WIDTH), row),
            pl.BlockSpec((MID_TM, ATTN_WIDTH), row),
            full(w_out), full(g_cross), full(w_cq), mem_spec, mem_spec, full(w_co), full(g_ffn),
        ],
        out_specs=[pl.BlockSpec((MID_TM, D_MODEL), row), pl.BlockSpec((D_MODEL, MID_TM), lambda i: (0, i))],
        out_shape=[jax.ShapeDtypeStruct((T, D_MODEL), F32), jax.ShapeDtypeStruct((D_MODEL, T), BF16)],
        compiler_params=_cparams("arbitrary"),
        name="mid",
    )(x2d, pool2d, attn2d, w_out, g_cross, w_cq, mem_k, mem_v, w_co, g_ffn)


ROUTER_TB = 256
ROUTER_GROUP = 4


def _top16(scores, with_index):
    row = lax.broadcasted_iota(jnp.int32, scores[0].shape, 0)
    vals = [[] for _ in scores]
    idxs = [[] for _ in scores]
    for _ in range(PEER_TOPK):
        m = [jnp.max(s, axis=0, keepdims=True) for s in scores]
        if with_index:
            idx = [jnp.min(jnp.where(s == mx, row, PEER_KEYS), axis=0, keepdims=True)
                   for s, mx in zip(scores, m)]
            scores = [jnp.where(row == ix, NEG_INF, s) for s, ix in zip(scores, idx)]
        else:
            idx = m
            scores = [jnp.where(s == mx, NEG_INF, s) for s, mx in zip(scores, m)]
        for k in range(len(scores)):
            vals[k].append(m[k])
            idxs[k].append(idx[k])
    cat = lambda rows: jnp.concatenate(rows, axis=0)
    if with_index:
        return [(cat(v), cat(i), None) for v, i in zip(vals, idxs)]
    removed = [jnp.sum(jnp.where(s == NEG_INF, 1, 0), axis=0, keepdims=True) for s in scores]
    return [(cat(v), None, r == PEER_TOPK) for v, r in zip(vals, removed)]


def _router_head(s1, s2, with_index):
    n = s1.shape[1]
    (v1, i1, ok1), (v2, i2, ok2) = _top16([s1, s2], with_index)
    row = lax.broadcasted_iota(jnp.int32, (PEER_TOPK, n), 0)
    top = v1[0:1] + v2[0:1]
    front = v1 + v2[0:1]
    cnt = jnp.zeros((PEER_TOPK, n), jnp.int32)
    zsum = jnp.zeros((1, n), F32)
    for _ in range(PEER_TOPK):
        m = jnp.max(front, axis=0, keepdims=True)
        r = jnp.min(jnp.where(front == m, row, PEER_TOPK), axis=0, keepdims=True)
        pick = row == r
        cnt = jnp.where(pick, cnt + 1, cnt)
        used = jnp.max(jnp.where(pick, cnt, 0), axis=0, keepdims=True)
        nxt = jnp.max(jnp.where(row == used, v2, NEG_INF), axis=0, keepdims=True)
        front = jnp.where(pick, v1 + nxt, front)
        zsum = zsum + jnp.exp(m - top)
    key = lax.broadcasted_iota(jnp.int32, (PEER_KEYS, n), 0)
    count = jnp.zeros((PEER_KEYS, n), jnp.int32)
    rank2 = jnp.full((PEER_KEYS, n), PEER_KEYS, jnp.int32)
    for r in range(PEER_TOPK):
        at1 = key == i1[r:r + 1] if with_index else s1 == v1[r:r + 1]
        at2 = key == i2[r:r + 1] if with_index else s2 == v2[r:r + 1]
        count = jnp.where(at1, cnt[r:r + 1], count)
        rank2 = jnp.where(at2, r, rank2)
    coef = jnp.exp(s1 - v1[0:1]) * (0.5 / zsum)
    e2 = jnp.exp(s2 - v2[0:1])
    ok = None if with_index else jnp.logical_and(ok1, ok2)
    return count.astype(F32), coef, rank2.astype(F32), e2, ok


def _router_kernel(hft_ref, wpq_ref, k1_ref, k2_ref, count_ref, coef_ref, rank2_ref, e2_ref, q_scr):
    q_scr[...] = lax.dot_general(wpq_ref[...], hft_ref[...], (((0,), (0,)), ((), ())),
                                 preferred_element_type=F32).astype(BF16)
    tb = hft_ref.shape[1]
    blocks = [(h, c) for h in range(PEER_HEADS) for c in range(tb // LANES)]
    for g in range(0, len(blocks), ROUTER_GROUP):
        group = blocks[g:g + ROUTER_GROUP]
        scores = []
        for h, c in group:
            base = h * 2 * PEER_HALF
            ls = slice(c * LANES, (c + 1) * LANES)
            scores.append((_dot(k1_ref[...], q_scr[base:base + PEER_HALF, ls]),
                           _dot(k2_ref[...], q_scr[base + PEER_HALF:base + 2 * PEER_HALF, ls])))

        def emit(with_index):
            oks = []
            for (h, c), (s1, s2) in zip(group, scores):
                ls = slice(c * LANES, (c + 1) * LANES)
                count, coef, rank2, e2, ok = _router_head(s1, s2, with_index)
                count_ref[h, :, ls] = count
                coef_ref[h, :, ls] = coef
                rank2_ref[h, :, ls] = rank2.astype(BF16)
                e2_ref[h, :, ls] = e2.astype(BF16)
                oks.append(ok)
            return oks

        oks = emit(False)
        tied = sum(jnp.sum(jnp.where(ok, 0, 1)) for ok in oks) > 0

        @pl.when(tied)
        def _():
            emit(True)


def _router(hft, w_pq, k1, k2):
    T = hft.shape[1]
    full = lambda a: pl.BlockSpec(a.shape, lambda i: (0, 0))
    out_spec = pl.BlockSpec((PEER_HEADS, PEER_KEYS, ROUTER_TB), lambda i: (0, 0, i))
    return pl.pallas_call(
        _router_kernel,
        grid=(T // ROUTER_TB,),
        in_specs=[pl.BlockSpec((D_MODEL, ROUTER_TB), lambda i: (0, i)), full(w_pq), full(k1), full(k2)],
        out_specs=[out_spec] * 4,
        out_shape=[jax.ShapeDtypeStruct((PEER_HEADS, PEER_KEYS, T), dt) for dt in (F32, F32, BF16, BF16)],
        scratch_shapes=[pltpu.VMEM((D_MODEL, ROUTER_TB), BF16)],
        compiler_params=_cparams("arbitrary"),
        name="router",
    )(hft, w_pq, k1, k2)


PEER_TB = 1024
PEER_E1 = 8
PEER_TE = PEER_E1 * PEER_KEYS


def _gelu_gate(x, half_gate):
    c0 = math.sqrt(2.0 / math.pi)
    inner = x * (c0 + (c0 * 0.044715) * (x * x))
    return (x * half_gate) * (1.0 + jnp.tanh(inner))


def _peer_kernel(hft_ref, wu_lo_ref, wu_hi_ref, wv_lo_ref, wv_hi_ref, count_ref, coef_ref, rank2_ref,
                 e2_ref, yt_ref, at_scr, p_scr):
    j = pl.program_id(1)

    @pl.when(j == 0)
    def _():
        yt_ref[...] = jnp.zeros_like(yt_ref)

    half_e = PEER_TE // 2
    at_scr[0:half_e, :] = _dot(wu_lo_ref[...], hft_ref[...])
    at_scr[half_e:, :] = _dot(wu_hi_ref[...], hft_ref[...])
    tb = hft_ref.shape[1]
    pack = BF16_SUBLANES
    groups = (PEER_KEYS // pack, pack, LANES)
    for e in range(PEER_E1):
        rows = slice(e * PEER_KEYS, (e + 1) * PEER_KEYS)
        cnt = [jnp.broadcast_to(count_ref[h, e:e + 1, :], (pack, tb)).astype(BF16) for h in range(PEER_HEADS)]
        cf = [jnp.broadcast_to(coef_ref[h, e:e + 1, :], (pack, tb)).astype(BF16) for h in range(PEER_HEADS)]
        for c in range(tb // LANES):
            ls = slice(c * LANES, (c + 1) * LANES)
            gate = None
            for h in range(PEER_HEADS):
                hit = rank2_ref[h, :, ls].reshape(groups) < cnt[h][None, :, ls]
                term = jnp.where(hit, e2_ref[h, :, ls].reshape(groups), 0.0) * cf[h][None, :, ls]
                gate = term if gate is None else gate + term
            gate = gate.astype(F32).reshape(PEER_KEYS, LANES)
            p_scr[rows, ls] = _gelu_gate(at_scr[rows, ls], gate).astype(BF16)
    half_d = D_MODEL // 2
    tn = lambda w, p: lax.dot_general(w, p, (((0,), (0,)), ((), ())), preferred_element_type=F32)
    yt_ref[0:half_d, :] += tn(wv_lo_ref[...], p_scr[...])
    yt_ref[half_d:, :] += tn(wv_hi_ref[...], p_scr[...])


def _peer(hft, w_u, w_v, count, coef, rank2, e2):
    T = hft.shape[1]
    once = pl.Buffered(1)
    tok = pl.BlockSpec((PEER_HEADS, PEER_KEYS, PEER_TB), lambda i, j: (0, 0, i), pipeline_mode=once)
    per_e1 = pl.BlockSpec((PEER_HEADS, PEER_E1, PEER_TB), lambda i, j: (0, j, i))
    return pl.pallas_call(
        _peer_kernel,
        grid=(T // PEER_TB, w_u.shape[0] // PEER_TE),
        in_specs=[
            pl.BlockSpec((D_MODEL, PEER_TB), lambda i, j: (0, i), pipeline_mode=once),
            pl.BlockSpec((PEER_TE // 2, D_MODEL), lambda i, j: (2 * j, 0)),
            pl.BlockSpec((PEER_TE // 2, D_MODEL), lambda i, j: (2 * j + 1, 0)),
            pl.BlockSpec((PEER_TE, D_MODEL // 2), lambda i, j: (j, 0)),
            pl.BlockSpec((PEER_TE, D_MODEL // 2), lambda i, j: (j, 1)),
            per_e1, per_e1, tok, tok,
        ],
        out_specs=pl.BlockSpec((D_MODEL, PEER_TB), lambda i, j: (0, i)),
        out_shape=jax.ShapeDtypeStruct((D_MODEL, T), F32),
        scratch_shapes=[pltpu.VMEM((PEER_TE, PEER_TB), F32), pltpu.VMEM((PEER_TE, PEER_TB), BF16)],
        compiler_params=_cparams("arbitrary", "arbitrary"),
        name="peer",
    )(hft, w_u, w_u, w_v, w_v, count, coef, rank2, e2)


FINAL_TM = 512


def _final_kernel(x2_ref, yt_ref, g_ref, o_ref, *, norm):
    x3 = x2_ref[...] + yt_ref[...].T
    o_ref[...] = _rms(x3, g_ref[...]) if norm else x3


def _final(x2, yt, g_final, norm=True):
    T = x2.shape[0]
    return pl.pallas_call(
        functools.partial(_final_kernel, norm=norm),
        grid=(T // FINAL_TM,),
        in_specs=[
            pl.BlockSpec((FINAL_TM, D_MODEL), lambda i: (i, 0)),
            pl.BlockSpec((D_MODEL, FINAL_TM), lambda i: (0, i)),
            pl.BlockSpec((1, D_MODEL), lambda i: (0, 0)),
        ],
        out_specs=pl.BlockSpec((FINAL_TM, D_MODEL), lambda i: (i, 0)),
        out_shape=jax.ShapeDtypeStruct((T, D_MODEL), F32),
        compiler_params=_cparams("arbitrary"),
        name="final",
    )(x2, yt, g_final)


def _layer(x2d, mem2d, pos, inv_lane, B, S, g_mix, w_in, w_pool, pool_scale, w_out, g_cross, g_mem,
           w_cq, w_ck, w_cv, w_co, g_ffn, w_pq, sub_keys_1, sub_keys_2, w_u, w_v):
    row = lambda g: g.reshape(1, -1)
    p, q, k, v = _in_proj(x2d, row(g_mix), pos, inv_lane, w_in.astype(BF16))
    pool = _pool(p.reshape(B, S, POOL_WIDTH), w_pool.astype(BF16), pool_scale)
    to3d = lambda a: a.reshape(B, S, ATTN_WIDTH)
    attn = _dilated_attention(to3d(q), to3d(k), to3d(v))
    mem_k, mem_v = _mem_kv(mem2d, row(g_mem), w_ck.astype(BF16), w_cv.astype(BF16))
    mem_len = mem2d.shape[0] // B
    x2, hft = _mid(x2d, pool.reshape(B * S, POOL_WIDTH), attn.reshape(B * S, ATTN_WIDTH),
                   w_out.astype(BF16), row(g_cross), w_cq.astype(BF16),
                   mem_k.reshape(B, mem_len, CROSS_WIDTH), mem_v.reshape(B, mem_len, CROSS_WIDTH),
                   w_co.astype(BF16), row(g_ffn), S)
    count, coef, rank2, e2 = _router(hft, w_pq.astype(BF16), sub_keys_1.astype(BF16),
                                     sub_keys_2.astype(BF16))
    yt = _peer(hft, w_u.astype(BF16), w_v.astype(BF16), count, coef, rank2, e2)
    return x2, yt


def kernel(x, mem, positions, g_mix, w_in, w_pool, pool_scale, w_out, g_cross, g_mem, w_cq, w_ck, w_cv,
           w_co, g_ffn, w_pq, sub_keys_1, sub_keys_2, w_u, w_v, g_final):
    B, S, D = x.shape
    depth = w_in.shape[0]
    lane = jnp.arange(LANES)
    inv = ROPE_THETA ** (-(2.0 * (lane % ROPE_HALF)).astype(F32) / ROPE_DIM)
    inv_lane = jnp.where(lane < ROPE_DIM, inv, 0.0).astype(F32).reshape(1, LANES)
    pos = positions.reshape(B * S, 1)
    x2d = x.reshape(B * S, D)
    mem2d = mem.reshape(-1, D)
    for i in range(depth):
        x2, yt = _layer(x2d, mem2d, pos, inv_lane, B, S, g_mix[i], w_in[i], w_pool[i], pool_scale[i],
                        w_out[i], g_cross[i], g_mem[i], w_cq[i], w_ck[i], w_cv[i], w_co[i], g_ffn[i],
                        w_pq[i], sub_keys_1[i], sub_keys_2[i], w_u[i], w_v[i])
        if i + 1 < depth:
            x2d = _final(x2, yt, g_final.reshape(1, D), norm=False)
    return _final(x2, yt, g_final.reshape(1, D)).reshape(B, S, D)
```
